```python
import math
import jax, jax.numpy as jnp
from jax import lax
import numpy as np

D_MODEL = 1024
BATCH = 8
SEQ = 2048
DEPTH = 2

GRID_W = 64
NA_HEADS = 8
NA_HEAD_DIM = 64
NA_WIN_ROWS = 8
NA_WIN_COLS = 16
NA_QCOL_BLOCK = NA_WIN_COLS
NA_KCOL_BLOCK = 2 * NA_WIN_COLS
SW_HEADS = 8
SW_KV_HEADS = 2
SW_HEAD_DIM = 64
SW_WINDOW = 128
SW_BLOCK = 128
REL_BUCKETS = 32
REL_MAX_DIST = 128
D_FF = 2816
N_BRANCHES = 2
EPS = 1e-6
NEG = -1e30

NA_WIDTH = NA_HEADS * NA_HEAD_DIM
SW_Q_WIDTH = SW_HEADS * SW_HEAD_DIM
SW_KV_WIDTH = SW_KV_HEADS * SW_HEAD_DIM
IN_WIDTH = 3 * NA_WIDTH + SW_Q_WIDTH + 2 * SW_KV_WIDTH + N_BRANCHES * D_MODEL

kernel_name = "hybrid_natten_swa_macaron_encoder"


def rms_norm(x, g):
    xf = x.astype(jnp.float32)
    y = xf * lax.rsqrt(jnp.mean(xf * xf, axis=-1, keepdims=True) + EPS)
    return (y * g.astype(jnp.float32)).astype(x.dtype)


def swiglu(x, w_gate, w_up, w_down):
    return (jax.nn.silu(x @ w_gate) * (x @ w_up)) @ w_down


def t5_bucket(rel):
    nb = REL_BUCKETS // 2
    max_exact = nb // 2
    n = np.abs(rel)
    large = max_exact + (np.log(np.maximum(n, 1) / max_exact)
                         / np.log(REL_MAX_DIST / max_exact) * (nb - max_exact)).astype(np.int32)
    large = np.minimum(large, nb - 1)
    return ((rel > 0) * nb + np.where(n < max_exact, n, large)).astype(np.int32)


def neighbourhood_attention(q, k, v, rpb):
    B, S, H, dh = q.shape
    rows = S // GRID_W
    kr = min(NA_WIN_ROWS, rows)
    ncb = GRID_W // NA_QCOL_BLOCK
    r = np.arange(rows)
    row_start = np.clip(r - kr // 2, 0, rows - kr)
    key_rows = row_start[:, None] + np.arange(kr)
    cb = np.arange(ncb)
    kcol_start = np.clip(cb * NA_QCOL_BLOCK - NA_WIN_COLS // 2, 0, GRID_W - NA_KCOL_BLOCK)
    key_cols = kcol_start[:, None] + np.arange(NA_KCOL_BLOCK)
    key_idx = (key_rows[:, None, :, None] * GRID_W + key_cols[None, :, None, :])
    key_idx = key_idx.reshape(rows, ncb, kr * NA_KCOL_BLOCK)
    kg = k[:, key_idx]
    vg = v[:, key_idx]
    qb = q.reshape(B, rows, ncb, NA_QCOL_BLOCK, H, dh)
    s = jnp.einsum('brcqhd,brckhd->bhrcqk', qb, kg).astype(jnp.float32) / math.sqrt(dh)
    q_cols = cb[:, None] * NA_QCOL_BLOCK + np.arange(NA_QCOL_BLOCK)
    q_col_start = np.clip(q_cols - NA_WIN_COLS // 2, 0, GRID_W - NA_WIN_COLS)
    kc = key_cols[:, None, :]
    col_ok = (kc >= q_col_start[..., None]) & (kc < q_col_start[..., None] + NA_WIN_COLS)
    row_idx = key_rows - r[:, None] + NA_WIN_ROWS - 1
    col_idx = np.clip(kc - q_cols[..., None] + NA_WIN_COLS - 1, 0, 2 * NA_WIN_COLS - 2)
    bias = rpb[:, row_idx[:, None, None, :, None], col_idx[None, :, :, None, :]]
    bias = bias.reshape(H, rows, ncb, NA_QCOL_BLOCK, kr * NA_KCOL_BLOCK)
    mask = np.broadcast_to(col_ok[None, :, :, None, :],
                           (rows, ncb, NA_QCOL_BLOCK, kr, NA_KCOL_BLOCK))
    mask = mask.reshape(rows, ncb, NA_QCOL_BLOCK, kr * NA_KCOL_BLOCK)
    s = jnp.where(mask, s + bias.astype(jnp.float32), NEG)
    p = jax.nn.softmax(s, axis=-1)
    o = jnp.einsum('bhrcqk,brckhd->brcqhd', p.astype(v.dtype), vg)
    return o.reshape(B, S, H * dh)


def sliding_window_gqa(q, k, v, rel_bias, sink):
    B, S, _, dh = q.shape
    nb = S // SW_BLOCK
    rep = SW_HEADS // SW_KV_HEADS
    qb = q.reshape(B, nb, SW_BLOCK, SW_KV_HEADS, rep, dh)
    pad = ((0, 0), (SW_BLOCK, SW_BLOCK), (0, 0), (0, 0))
    kp = jnp.pad(k, pad).reshape(B, nb + 2, SW_BLOCK, SW_KV_HEADS, dh)
    vp = jnp.pad(v, pad).reshape(B, nb + 2, SW_BLOCK, SW_KV_HEADS, dh)
    kb = jnp.concatenate([kp[:, :-2], kp[:, 1:-1], kp[:, 2:]], axis=2)
    vb = jnp.concatenate([vp[:, :-2], vp[:, 1:-1], vp[:, 2:]], axis=2)
    s = jnp.einsum('bnqgrd,bnkgd->bgrnqk', qb, kb).astype(jnp.float32) / math.sqrt(dh)
    a = np.arange(SW_BLOCK)[:, None]
    j = np.arange(3 * SW_BLOCK)[None, :]
    rel = j - SW_BLOCK - a
    kpos = (np.arange(nb)[:, None, None] - 1) * SW_BLOCK + j[None]
    mask = (np.abs(rel)[None] <= SW_WINDOW) & (kpos >= 0) & (kpos < S)
    bias = rel_bias.astype(jnp.float32).reshape(SW_KV_HEADS, rep, 1, SW_BLOCK, 3 * SW_BLOCK)
    s = jnp.where(mask, s + bias, NEG)
    sink_b = sink.astype(jnp.float32).reshape(SW_KV_HEADS, rep, 1, 1, 1)
    m = jnp.maximum(s.max(axis=-1, keepdims=True), sink_b)
    e = jnp.exp(s - m)
    p = e / (e.sum(axis=-1, keepdims=True) + jnp.exp(sink_b - m))
    o = jnp.einsum('bgrnqk,bnkgd->bnqgrd', p.astype(v.dtype), vb)
    return o.reshape(B, S, SW_HEADS * dh)


def setup_inputs(seed: int = 0) -> dict:
    key = jax.random.key(seed)
    ks = jax.random.split(key, 24)
    L, D, F = DEPTH, D_MODEL, D_FF
    nrm = lambda k, shape, fan: jax.random.normal(k, shape, jnp.float32) * fan ** -0.5
    gain = lambda k, shape: 1.0 + 0.05 * jax.random.normal(k, shape, jnp.float32)
    return {
        "x": jax.random.normal(ks[0], (BATCH, SEQ, D), jnp.float32),
        "ffn1_norm": gain(ks[1], (L, D)),
        "ffn1_w_gate": nrm(ks[2], (L, D, F), D),
        "ffn1_w_up": nrm(ks[3], (L, D, F), D),
        "ffn1_w_down": nrm(ks[4], (L, F, D), F),
        "mix_norm": gain(ks[5], (L, D)),
        "w_in": nrm(ks[6], (L, D, IN_WIDTH), D),
        "b_gate": 0.01 * jax.random.normal(ks[7], (L, N_BRANCHES * D), jnp.float32),
        "na_q_norm": gain(ks[8], (L, NA_HEAD_DIM)),
        "na_k_norm": gain(ks[9], (L, NA_HEAD_DIM)),
        "na_rpb": 0.1 * jax.random.normal(ks[10], (L, NA_HEADS, 2 * NA_WIN_ROWS - 1, 2 * NA_WIN_COLS - 1), jnp.float32),
        "sw_q_norm": gain(ks[11], (L, SW_HEAD_DIM)),
        "sw_k_norm": gain(ks[12], (L, SW_HEAD_DIM)),
        "sw_sink": 0.5 * jax.random.normal(ks[13], (L, SW_HEADS), jnp.float32),
        "t5_rel_table": 0.1 * jax.random.normal(ks[14], (REL_BUCKETS, SW_HEADS), jnp.float32),
        "w_branch_na": nrm(ks[15], (L, NA_WIDTH, D), NA_WIDTH),
        "w_branch_sw": nrm(ks[16], (L, SW_Q_WIDTH, D), SW_Q_WIDTH),
        "w_out": nrm(ks[17], (L, D, D), D),
        "ffn2_norm": gain(ks[18], (L, D)),
        "ffn2_w_gate": nrm(ks[19], (L, D, F), D),
        "ffn2_w_up": nrm(ks[20], (L, D, F), D),
        "ffn2_w_down": nrm(ks[21], (L, F, D), F),
    }


def reference(x, ffn1_norm, ffn1_w_gate, ffn1_w_up, ffn1_w_down, mix_norm, w_in, b_gate,
              na_q_norm, na_k_norm, na_rpb, sw_q_norm, sw_k_norm, sw_sink, t5_rel_table,
              w_branch_na, w_branch_sw, w_out, ffn2_norm, ffn2_w_gate, ffn2_w_up, ffn2_w_down):
    B, S, D = x.shape
    rel = np.arange(3 * SW_BLOCK)[None, :] - SW_BLOCK - np.arange(SW_BLOCK)[:, None]
    t5_bias = jnp.transpose(t5_rel_table[t5_bucket(rel)], (2, 0, 1))
    splits = np.cumsum([NA_WIDTH, NA_WIDTH, NA_WIDTH, SW_Q_WIDTH, SW_KV_WIDTH, SW_KV_WIDTH])
    for l in range(DEPTH):
        x = x + 0.5 * swiglu(rms_norm(x, ffn1_norm[l]), ffn1_w_gate[l], ffn1_w_up[l], ffn1_w_down[l])
        h = rms_norm(x, mix_norm[l])
        z = h @ w_in[l]
        qa, ka, va, qs, ks_, vs, zg = jnp.split(z, splits, axis=-1)
        qa = rms_norm(qa.reshape(B, S, NA_HEADS, NA_HEAD_DIM), na_q_norm[l])
        ka = rms_norm(ka.reshape(B, S, NA_HEADS, NA_HEAD_DIM), na_k_norm[l])
        va = va.reshape(B, S, NA_HEADS, NA_HEAD_DIM)
        o_na = neighbourhood_attention(qa, ka, va, na_rpb[l])
        qs = rms_norm(qs.reshape(B, S, SW_HEADS, SW_HEAD_DIM), sw_q_norm[l])
        ks_ = rms_norm(ks_.reshape(B, S, SW_KV_HEADS, SW_HEAD_DIM), sw_k_norm[l])
        vs = vs.reshape(B, S, SW_KV_HEADS, SW_HEAD_DIM)
        o_sw = sliding_window_gqa(qs, ks_, vs, t5_bias, sw_sink[l])
        g = jax.nn.sigmoid((zg + b_gate[l]).astype(jnp.float32)).astype(x.dtype)
        g = g.reshape(B, S, N_BRANCHES, D)
        merged = g[:, :, 0] * (o_na @ w_branch_na[l]) + g[:, :, 1] * (o_sw @ w_branch_sw[l])
        x = x + merged @ w_out[l]
        x = x + 0.5 * swiglu(rms_norm(x, ffn2_norm[l]), ffn2_w_gate[l], ffn2_w_up[l], ffn2_w_down[l])
    return x
```

```python
import functools
import math

import jax
import jax.numpy as jnp
import numpy as np
from jax import lax
from jax.experimental import pallas as pl
from jax.experimental.pallas import tpu as pltpu

F32 = jnp.float32
BF16 = jnp.bfloat16

D_MODEL = 1024
D_FF = 2816
GRID_W = 64
HEAD_DIM = 64
N_HEADS = 8
NA_WIN_ROWS = 8
NA_WIN_COLS = 16
SW_KV_HEADS = 2
SW_BLOCK = 128
SW_WINDOW = 128
REL_BUCKETS = 32
REL_MAX_DIST = 128
EPS = 1e-6
NEG = -1e30

ATT_W = N_HEADS * HEAD_DIM
KV_DUP_W = 2 * SW_KV_HEADS * HEAD_DIM
LANES = 128
FFN_CHUNK = 512
TOKEN_TILE = 512
VMEM_LIMIT = 56 * 1024 * 1024


def _dot(a, b):
    return jnp.dot(a, b, preferred_element_type=F32)


def _dot_nt(a, b):
    return lax.dot_general(a, b, (((1,), (1,)), ((), ())), preferred_element_type=F32)


def _rms(x, g):
    ms = jnp.mean(x * x, axis=-1, keepdims=True)
    return x * lax.rsqrt(ms + EPS) * g


def _ffn_update(x, norm_ref, wg_ref, wu_ref, wd_ref, act_ref):
    h = _rms(x, norm_ref[...]).astype(BF16)
    for c in range(0, D_FF, FFN_CHUNK):
        w = min(FFN_CHUNK, D_FF - c)
        g = _dot(h, wg_ref[:, c:c + w])
        u = _dot(h, wu_ref[:, c:c + w])
        act_ref[:, c:c + w] = (g * jax.nn.sigmoid(g) * u).astype(BF16)
    return x + 0.5 * _dot(act_ref[...], wd_ref[...])


def _head_norm(z, bd, gain):
    sq = z * z
    hi = sq.astype(BF16)
    lo = (sq - hi.astype(F32)).astype(BF16)
    ms = _dot(hi, bd) + _dot(lo, bd)
    return z * lax.rsqrt(ms + EPS) * gain


def _inproj(x1, mixnorm_ref, win_ref, bgate_ref, bd_ref, gains_ref,
            qa_ref, ka_ref, va_ref, qs_ref, ks_ref, vs_ref, g_ref):
    h = _rms(x1, mixnorm_ref[...]).astype(BF16)
    bd = bd_ref[...]
    w = ATT_W
    qa_ref[...] = _head_norm(_dot(h, win_ref[:, 0:w]), bd, gains_ref[0:1, :]).astype(BF16)
    ka_ref[...] = _head_norm(_dot(h, win_ref[:, w:2 * w]), bd, gains_ref[1:2, :]).astype(BF16)
    va_ref[...] = _dot(h, win_ref[:, 2 * w:3 * w]).astype(BF16)
    qs_ref[...] = _head_norm(_dot(h, win_ref[:, 3 * w:4 * w]), bd, gains_ref[2:3, :]).astype(BF16)
    c = 4 * w
    ks_ref[...] = _head_norm(_dot(h, win_ref[:, c:c + KV_DUP_W]), bd_ref[0:KV_DUP_W, 0:KV_DUP_W],
                             gains_ref[3:4, 0:KV_DUP_W]).astype(BF16)
    c += KV_DUP_W
    vs_ref[...] = _dot(h, win_ref[:, c:c + KV_DUP_W]).astype(BF16)
    c += KV_DUP_W
    for j in range(0, 2 * D_MODEL, FFN_CHUNK):
        zg = _dot(h, win_ref[:, c + j:c + j + FFN_CHUNK]) + bgate_ref[:, j:j + FFN_CHUNK]
        g_ref[:, j:j + FFN_CHUNK] = jax.nn.sigmoid(zg).astype(BF16)


def _ffn_inproj_kernel(x_ref, n1_ref, wg_ref, wu_ref, wd_ref, mixnorm_ref, win_ref, bgate_ref,
                       bd_ref, gains_ref,
                       x1_ref, qa_ref, ka_ref, va_ref, qs_ref, ks_ref, vs_ref, g_ref, act_ref):
    x1 = _ffn_update(x_ref[...], n1_ref, wg_ref, wu_ref, wd_ref, act_ref)
    x1_ref[...] = x1
    _inproj(x1, mixnorm_ref, win_ref, bgate_ref, bd_ref, gains_ref,
            qa_ref, ka_ref, va_ref, qs_ref, ks_ref, vs_ref, g_ref)


def _outproj_ffn_kernel(x_ref, ona_ref, osw_ref, g_ref, pa_ref, pb_ref, wout_ref,
                        n2_ref, wg_ref, wu_ref, wd_ref, o_ref, act_ref):
    ya = _dot(ona_ref[...], pa_ref[...])
    yb = _dot(osw_ref[...], pb_ref[...])
    merged = g_ref[:, 0:D_MODEL].astype(F32) * ya + g_ref[:, D_MODEL:2 * D_MODEL].astype(F32) * yb
    x2 = x_ref[...] + _dot(merged.astype(BF16), wout_ref[...])
    o_ref[...] = _ffn_update(x2, n2_ref, wg_ref, wu_ref, wd_ref, act_ref)


def _na_kernel(q_ref, k_ref, v_ref, bias_ref, o_ref, *, rows):
    kr = min(NA_WIN_ROWS, rows)
    lo_half = lax.broadcasted_iota(jnp.int32, (GRID_W, LANES), 1) < HEAD_DIM

    def row_body(r, carry):
        row_start = jnp.clip(r - kr // 2, 0, rows - kr)
        d = r - row_start
        q0 = pl.multiple_of(r * GRID_W, GRID_W)
        k0 = pl.multiple_of(row_start * GRID_W, GRID_W)
        for hp in range(N_HEADS // 2):
            cols = slice(hp * LANES, (hp + 1) * LANES)
            q2 = q_ref[pl.ds(q0, GRID_W), cols].astype(F32)
            k2 = k_ref[pl.ds(k0, kr * GRID_W), cols]
            v2 = v_ref[pl.ds(k0, kr * GRID_W), cols]
            outs = []
            for hh in range(2):
                h = 2 * hp + hh
                qm = (jnp.where(lo_half, q2, 0.0) if hh == 0 else jnp.where(lo_half, 0.0, q2))
                s = _dot_nt(qm.astype(BF16), k2) + bias_ref[h * NA_WIN_ROWS + d]
                m = jnp.max(s, axis=-1, keepdims=True)
                e = jnp.exp(s - m)
                l = jnp.sum(e, axis=-1, keepdims=True)
                outs.append(_dot(e.astype(BF16), v2) / l)
            o_ref[pl.ds(q0, GRID_W), cols] = jnp.where(lo_half, outs[0], outs[1]).astype(BF16)
        return carry

    lax.fori_loop(0, rows, row_body, 0)


def _sw_kernel(sink_ref, q_ref, k_ref, v_ref, bias_ref, o_ref, kpad_ref, vpad_ref, *, seq):
    nb = seq // SW_BLOCK
    rep = N_HEADS // SW_KV_HEADS
    zeros = jnp.zeros((SW_BLOCK, KV_DUP_W), BF16)
    for pad_ref, src_ref in ((kpad_ref, k_ref), (vpad_ref, v_ref)):
        pad_ref[0:SW_BLOCK, :] = zeros
        pad_ref[SW_BLOCK + seq:2 * SW_BLOCK + seq, :] = zeros
        pad_ref[SW_BLOCK:SW_BLOCK + seq, :] = src_ref[...]
    lo_half = lax.broadcasted_iota(jnp.int32, (SW_BLOCK, LANES), 1) < HEAD_DIM

    def block_body(n, carry):
        variant = jnp.where(n == 0, 0, jnp.where(n == nb - 1, 2, 1))
        q0 = pl.multiple_of(n * SW_BLOCK, SW_BLOCK)
        for g in range(SW_KV_HEADS):
            kv_cols = slice(g * LANES, (g + 1) * LANES)
            ks = kpad_ref[pl.ds(q0, 3 * SW_BLOCK), kv_cols]
            vs = vpad_ref[pl.ds(q0, 3 * SW_BLOCK), kv_cols]
            parts = []
            for jj in range(rep // 2):
                j = g * (rep // 2) + jj
                q2 = q_ref[pl.ds(q0, SW_BLOCK), j * LANES:(j + 1) * LANES].astype(F32)
                parts.append(jnp.where(lo_half, q2, 0.0))
                parts.append(jnp.where(lo_half, 0.0, q2))
            qst = jnp.concatenate(parts, axis=0).astype(BF16)
            s = _dot_nt(qst, ks)
            ps, ls = [], []
            for i in range(rep):
                h = g * rep + i
                si = s[i * SW_BLOCK:(i + 1) * SW_BLOCK, :] + bias_ref[variant * N_HEADS + h]
                sink = sink_ref[h]
                m = jnp.maximum(jnp.max(si, axis=-1, keepdims=True), sink)
                e = jnp.exp(si - m)
                ls.append(jnp.sum(e, axis=-1, keepdims=True) + jnp.exp(sink - m))
                ps.append(e.astype(BF16))
            res = _dot(jnp.concatenate(ps, axis=0), vs)
            for jj in range(rep // 2):
                j = g * (rep // 2) + jj
                o_even = res[(2 * jj) * SW_BLOCK:(2 * jj + 1) * SW_BLOCK, :] / ls[2 * jj]
                o_odd = res[(2 * jj + 1) * SW_BLOCK:(2 * jj + 2) * SW_BLOCK, :] / ls[2 * jj + 1]
                o_ref[pl.ds(q0, SW_BLOCK), j * LANES:(j + 1) * LANES] = (
                    jnp.where(lo_half, o_even, o_odd).astype(BF16))
        return carry

    lax.fori_loop(0, nb, block_body, 0)


def _resident(shape):
    return pl.BlockSpec(shape, lambda *_: (0,) * len(shape), pipeline_mode=pl.Buffered(1))


def _tok(width):
    return pl.BlockSpec((TOKEN_TILE, width), lambda i: (i, 0))


def _dense_params():
    return pltpu.CompilerParams(dimension_semantics=("arbitrary",), vmem_limit_bytes=VMEM_LIMIT)


def _ffn_inproj(x, n1, wg, wu, wd, mixnorm, win, bgate, bd, gains):
    t = x.shape[0]
    in_w = win.shape[1]
    out_shape = (
        jax.ShapeDtypeStruct((t, D_MODEL), F32),
        jax.ShapeDtypeStruct((t, ATT_W), BF16), jax.ShapeDtypeStruct((t, ATT_W), BF16),
        jax.ShapeDtypeStruct((t, ATT_W), BF16), jax.ShapeDtypeStruct((t, ATT_W), BF16),
        jax.ShapeDtypeStruct((t, KV_DUP_W), BF16), jax.ShapeDtypeStruct((t, KV_DUP_W), BF16),
        jax.ShapeDtypeStruct((t, 2 * D_MODEL), BF16),
    )
    return pl.pallas_call(
        _ffn_inproj_kernel,
        out_shape=out_shape,
        grid=(t // TOKEN_TILE,),
        in_specs=[_tok(D_MODEL), _resident((1, D_MODEL)), _resident((D_MODEL, D_FF)),
                  _resident((D_MODEL, D_FF)), _resident((D_FF, D_MODEL)), _resident((1, D_MODEL)),
                  _resident((D_MODEL, in_w)), _resident((1, 2 * D_MODEL)),
                  _resident((ATT_W, ATT_W)), _resident((4, ATT_W))],
        out_specs=(_tok(D_MODEL), _tok(ATT_W), _tok(ATT_W), _tok(ATT_W), _tok(ATT_W),
                   _tok(KV_DUP_W), _tok(KV_DUP_W), _tok(2 * D_MODEL)),
        scratch_shapes=[pltpu.VMEM((TOKEN_TILE, D_FF), BF16)],
        compiler_params=_dense_params(),
        name="ffn_inproj",
    )(x, n1, wg, wu, wd, mixnorm, win, bgate, bd, gains)


def _outproj_ffn(x1, ona, osw, g, pa, pb, wout, n2, wg, wu, wd):
    t = x1.shape[0]
    return pl.pallas_call(
        _outproj_ffn_kernel,
        out_shape=jax.ShapeDtypeStruct((t, D_MODEL), F32),
        grid=(t // TOKEN_TILE,),
        in_specs=[_tok(D_MODEL), _tok(ATT_W), _tok(ATT_W), _tok(2 * D_MODEL),
                  _resident((ATT_W, D_MODEL)), _resident((ATT_W, D_MODEL)),
                  _resident((D_MODEL, D_MODEL)), _resident((1, D_MODEL)),
                  _resident((D_MODEL, D_FF)), _resident((D_MODEL, D_FF)),
                  _resident((D_FF, D_MODEL))],
        out_specs=_tok(D_MODEL),
        scratch_shapes=[pltpu.VMEM((TOKEN_TILE, D_FF), BF16)],
        compiler_params=_dense_params(),
        name="outproj_ffn",
    )(x1, ona, osw, g, pa, pb, wout, n2, wg, wu, wd)


def _seq_spec(width):
    return pl.BlockSpec((None, None, width), lambda b: (b, 0, 0))


def _na_attn(q, k, v, bias, batch, seq):
    rows = seq // GRID_W
    spec = pl.BlockSpec((None, seq, ATT_W), lambda b: (b, 0, 0))
    return pl.pallas_call(
        functools.partial(_na_kernel, rows=rows),
        out_shape=jax.ShapeDtypeStruct((batch, seq, ATT_W), BF16),
        grid=(batch,),
        in_specs=[spec, spec, spec, _resident(bias.shape)],
        out_specs=spec,
        compiler_params=_dense_params(),
        name="na_attn",
    )(q, k, v, bias)


def _sw_attn(sink, q, k, v, bias, batch, seq):
    qspec = pl.BlockSpec((None, seq, ATT_W), lambda b: (b, 0, 0))
    kvspec = pl.BlockSpec((None, seq, KV_DUP_W), lambda b: (b, 0, 0))
    return pl.pallas_call(
        functools.partial(_sw_kernel, seq=seq),
        out_shape=jax.ShapeDtypeStruct((batch, seq, ATT_W), BF16),
        grid=(batch,),
        in_specs=[pl.BlockSpec(memory_space=pltpu.SMEM), qspec, kvspec, kvspec,
                  _resident(bias.shape)],
        out_specs=qspec,
        scratch_shapes=[pltpu.VMEM((seq + 2 * SW_BLOCK, KV_DUP_W), BF16),
                        pltpu.VMEM((seq + 2 * SW_BLOCK, KV_DUP_W), BF16)],
        compiler_params=_dense_params(),
        name="sw_attn",
    )(sink, q, k, v, bias)


def _t5_bucket(rel):
    nb = REL_BUCKETS // 2
    max_exact = nb // 2
    n = np.abs(rel)
    large = max_exact + (np.log(np.maximum(n, 1) / max_exact)
                         / np.log(REL_MAX_DIST / max_exact) * (nb - max_exact)).astype(np.int32)
    large = np.minimum(large, nb - 1)
    return ((rel > 0) * nb + np.where(n < max_exact, n, large)).astype(np.int32)


def _sw_bias_tables(t5_rel_table, seq):
    a = np.arange(SW_BLOCK)[:, None]
    j = np.arange(3 * SW_BLOCK)[None, :]
    rel = j - SW_BLOCK - a
    band = np.abs(rel) <= SW_WINDOW
    nb = seq // SW_BLOCK
    masks = []
    for n in (0, 1 if nb > 2 else 0, nb - 1):
        kpos = (n - 1) * SW_BLOCK + j
        masks.append(band & (kpos >= 0) & (kpos < seq))
    mask = np.stack(masks)[:, None]
    t5 = jnp.transpose(t5_rel_table[_t5_bucket(rel)], (2, 0, 1))
    bias = jnp.where(mask, t5[None].astype(F32), NEG)
    return bias.reshape(3 * N_HEADS, SW_BLOCK, 3 * SW_BLOCK)


def _na_bias_tables(rpb, rows):
    kr = min(NA_WIN_ROWS, rows)
    d = np.arange(NA_WIN_ROWS)[:, None]
    ridx = np.clip(NA_WIN_ROWS - 1 - d + np.arange(kr)[None, :], 0, 2 * NA_WIN_ROWS - 2)
    c = np.arange(GRID_W)[:, None]
    kc = np.arange(GRID_W)[None, :]
    cidx = np.clip(kc - c + NA_WIN_COLS - 1, 0, 2 * NA_WIN_COLS - 2)
    start = np.clip(c - NA_WIN_COLS // 2, 0, GRID_W - NA_WIN_COLS)
    mask = (kc >= start) & (kc < start + NA_WIN_COLS)
    t = rpb.astype(F32)[:, ridx]
    t = t[..., cidx]
    t = jnp.transpose(t, (0, 1, 3, 2, 4))
    t = jnp.where(mask[None, None, :, None, :], t, NEG)
    return t.reshape(N_HEADS * NA_WIN_ROWS, GRID_W, kr * GRID_W)


def kernel(x, ffn1_norm, ffn1_w_gate, ffn1_w_up, ffn1_w_down, mix_norm, w_in, b_gate, na_q_norm, na_k_norm, na_rpb, sw_q_norm, sw_k_norm, sw_sink, t5_rel_table, w_branch_na, w_branch_sw, w_out, ffn2_norm, ffn2_w_gate, ffn2_w_up, ffn2_w_down):
    batch, seq, d_model = x.shape
    depth = w_in.shape[0]
    assert d_model == D_MODEL and seq % SW_BLOCK == 0 and seq % GRID_W == 0
    assert (batch * seq) % TOKEN_TILE == 0
    rows = seq // GRID_W
    scale = 1.0 / math.sqrt(HEAD_DIM)

    head_of = np.arange(ATT_W) // HEAD_DIM
    bd = jnp.asarray((head_of[:, None] == head_of[None, :]) / HEAD_DIM, BF16)
    sw_bias = _sw_bias_tables(t5_rel_table, seq)

    xt = x.reshape(batch * seq, D_MODEL)
    for l in range(depth):
        wl = w_in[l]
        a = ATT_W
        ks_w = wl[:, 4 * a:4 * a + 2 * HEAD_DIM]
        vs_w = wl[:, 4 * a + 2 * HEAD_DIM:4 * a + 4 * HEAD_DIM]
        dup = lambda w: jnp.concatenate([w[:, :HEAD_DIM], w[:, :HEAD_DIM],
                                         w[:, HEAD_DIM:], w[:, HEAD_DIM:]], axis=1)
        win = jnp.concatenate([wl[:, :4 * a], dup(ks_w), dup(vs_w), wl[:, 4 * a + 4 * HEAD_DIM:]],
                              axis=1).astype(BF16)
        gains = jnp.stack([jnp.tile(na_q_norm[l], N_HEADS) * scale,
                           jnp.tile(na_k_norm[l], N_HEADS),
                           jnp.tile(sw_q_norm[l], N_HEADS) * scale,
                           jnp.tile(sw_k_norm[l], N_HEADS)]).astype(F32)
        na_bias = _na_bias_tables(na_rpb[l], rows)

        x1, qa, ka, va, qs, ks, vs, g = _ffn_inproj(
            xt, ffn1_norm[l][None], ffn1_w_gate[l].astype(BF16), ffn1_w_up[l].astype(BF16),
            ffn1_w_down[l].astype(BF16), mix_norm[l][None], win, b_gate[l][None], bd, gains)
        to_seq = lambda t: t.reshape(batch, seq, t.shape[-1])
        ona = _na_attn(to_seq(qa), to_seq(ka), to_seq(va), na_bias, batch, seq)
        osw = _sw_attn(sw_sink[l], to_seq(qs), to_seq(ks), to_seq(vs), sw_bias, batch, seq)
        xt = _outproj_ffn(
            x1, ona.reshape(batch * seq, ATT_W), osw.reshape(batch * seq, ATT_W), g,
            w_branch_na[l].astype(BF16), w_branch_sw[l].astype(BF16), w_out[l].astype(BF16),
            ffn2_norm[l][None], ffn2_w_gate[l].astype(BF16), ffn2_w_up[l].astype(BF16),
            ffn2_w_down[l].astype(BF16))
    return xt.reshape(batch, seq, D_MODEL)
```

```python
import functools
import math

import jax
import jax.numpy as jnp
import numpy as np
from jax import lax
from jax.experimental import pallas as pl
from jax.experimental.pallas import tpu as pltpu

F32 = jnp.float32
BF16 = jnp.bfloat16

D_MODEL = 1024
D_FF = 2816
GRID_W = 64
HEAD_DIM = 64
N_HEADS = 8
NA_WIN_ROWS = 8
NA_WIN_COLS = 16
SW_KV_HEADS = 2
SW_BLOCK = 128
SW_WINDOW = 128
REL_BUCKETS = 32
REL_MAX_DIST = 128
EPS = 1e-6
NEG = -1e30

ATT_W = N_HEADS * HEAD_DIM
KV_DUP_W = 2 * SW_KV_HEADS * HEAD_DIM
LANES = 128
FFN_CHUNK = 512
TOKEN_TILE = 512
VMEM_LIMIT = 56 * 1024 * 1024


def _dot(a, b):
    return jnp.dot(a, b, preferred_element_type=F32)


def _dot_nt(a, b):
    return lax.dot_general(a, b, (((1,), (1,)), ((), ())), preferred_element_type=F32)


def _rms(x, g):
    ms = jnp.mean(x * x, axis=-1, keepdims=True)
    return x * lax.rsqrt(ms + EPS) * g


def _ffn_update(x, norm_ref, wg_ref, wu_ref, wd_ref, act_ref):
    h = _rms(x, norm_ref[...]).astype(BF16)
    for c in range(0, D_FF, FFN_CHUNK):
        w = min(FFN_CHUNK, D_FF - c)
        g = _dot(h, wg_ref[:, c:c + w])
        u = _dot(h, wu_ref[:, c:c + w])
        act_ref[:, c:c + w] = (g * jax.nn.sigmoid(g) * u).astype(BF16)
    return x + 0.5 * _dot(act_ref[...], wd_ref[...])


def _head_norm(z, bd, gain):
    sq = z * z
    hi = sq.astype(BF16)
    lo = (sq - hi.astype(F32)).astype(BF16)
    ms = _dot(hi, bd) + _dot(lo, bd)
    return z * lax.rsqrt(ms + EPS) * gain


def _inproj(x1, mixnorm_ref, win_ref, bgate_ref, bd_ref, gains_ref,
            qa_ref, ka_ref, va_ref, qs_ref, ks_ref, vs_ref, g_ref):
    h = _rms(x1, mixnorm_ref[...]).astype(BF16)
    bd = bd_ref[...]
    w = ATT_W
    qa_ref[...] = _head_norm(_dot(h, win_ref[:, 0:w]), bd, gains_ref[0:1, :]).astype(BF16)
    ka_ref[...] = _head_norm(_dot(h, win_ref[:, w:2 * w]), bd, gains_ref[1:2, :]).astype(BF16)
    va_ref[...] = _dot(h, win_ref[:, 2 * w:3 * w]).astype(BF16)
    qs_ref[...] = _head_norm(_dot(h, win_ref[:, 3 * w:4 * w]), bd, gains_ref[2:3, :]).astype(BF16)
    c = 4 * w
    ks_ref[...] = _head_norm(_dot(h, win_ref[:, c:c + KV_DUP_W]), bd_ref[0:KV_DUP_W, 0:KV_DUP_W],
                             gains_ref[3:4, 0:KV_DUP_W]).astype(BF16)
    c += KV_DUP_W
    vs_ref[...] = _dot(h, win_ref[:, c:c + KV_DUP_W]).astype(BF16)
    c += KV_DUP_W
    for j in range(0, 2 * D_MODEL, FFN_CHUNK):
        zg = _dot(h, win_ref[:, c + j:c + j + FFN_CHUNK]) + bgate_ref[:, j:j + FFN_CHUNK]
        g_ref[:, j:j + FFN_CHUNK] = jax.nn.sigmoid(zg).astype(BF16)


def _ffn_inproj_kernel(x_ref, n1_ref, wg_ref, wu_ref, wd_ref, mixnorm_ref, win_ref, bgate_ref,
                       bd_ref, gains_ref,
                       x1_ref, qa_ref, ka_ref, va_ref, qs_ref, ks_ref, vs_ref, g_ref, act_ref):
    x1 = _ffn_update(x_ref[...], n1_ref, wg_ref, wu_ref, wd_ref, act_ref)
    x1_ref[...] = x1
    _inproj(x1, mixnorm_ref, win_ref, bgate_ref, bd_ref, gains_ref,
            qa_ref, ka_ref, va_ref, qs_ref, ks_ref, vs_ref, g_ref)


def _outproj_ffn_kernel(x_ref, ona_ref, osw_ref, g_ref, pa_ref, pb_ref, wout_ref,
                        n2_ref, wg_ref, wu_ref, wd_ref, o_ref, act_ref):
    ya = _dot(ona_ref[...], pa_ref[...])
    yb = _dot(osw_ref[...], pb_ref[...])
    merged = g_ref[:, 0:D_MODEL].astype(F32) * ya + g_ref[:, D_MODEL:2 * D_MODEL].astype(F32) * yb
    x2 = x_ref[...] + _dot(merged.astype(BF16), wout_ref[...])
    o_ref[...] = _ffn_update(x2, n2_ref, wg_ref, wu_ref, wd_ref, act_ref)


def _na_kernel(q_ref, k_ref, v_ref, bias_ref, o_ref, *, rows):
    kr = NA_WIN_ROWS
    nkeys = kr * GRID_W
    lo_half = lax.broadcasted_iota(jnp.int32, (GRID_W, LANES), 1) < HEAD_DIM

    def row_body(r, carry):
        row_start = jnp.clip(r - kr // 2, 0, rows - kr)
        d = r - row_start
        q0 = pl.multiple_of(r * GRID_W, GRID_W)
        k0 = pl.multiple_of(row_start * GRID_W, GRID_W)
        s_parts = []
        for hp in range(N_HEADS // 2):
            cols = slice(hp * LANES, (hp + 1) * LANES)
            q2 = q_ref[pl.ds(q0, GRID_W), cols].astype(F32)
            qst = jnp.concatenate([jnp.where(lo_half, q2, 0.0), jnp.where(lo_half, 0.0, q2)], axis=0)
            s_parts.append(_dot_nt(qst.astype(BF16), k_ref[pl.ds(k0, nkeys), cols]))
        bias = jnp.concatenate([bias_ref[NA_WIN_ROWS - 1 - d + 2 * t] for t in range(kr // 2)], axis=1)
        s = jnp.concatenate(s_parts, axis=0) + bias
        m = jnp.max(s, axis=-1, keepdims=True)
        e = jnp.exp(s - m)
        l = jnp.sum(e, axis=-1, keepdims=True)
        p = e.astype(BF16)
        for hp in range(N_HEADS // 2):
            cols = slice(hp * LANES, (hp + 1) * LANES)
            res = _dot(p[cols, :], v_ref[pl.ds(k0, nkeys), cols]) / l[cols, :]
            o_ref[pl.ds(q0, GRID_W), cols] = jnp.where(lo_half, res[:GRID_W], res[GRID_W:]).astype(BF16)
        return carry

    lax.fori_loop(0, rows, row_body, 0, unroll=2)


def _sw_kernel(sink_ref, q_ref, k_ref, v_ref, bias_ref, o_ref, kpad_ref, vpad_ref, *, seq):
    nb = seq // SW_BLOCK
    rep = N_HEADS // SW_KV_HEADS
    zeros = jnp.zeros((SW_BLOCK, KV_DUP_W), BF16)
    for pad_ref, src_ref in ((kpad_ref, k_ref), (vpad_ref, v_ref)):
        pad_ref[0:SW_BLOCK, :] = zeros
        pad_ref[SW_BLOCK + seq:2 * SW_BLOCK + seq, :] = zeros
        pad_ref[SW_BLOCK:SW_BLOCK + seq, :] = src_ref[...]
    lo_half = lax.broadcasted_iota(jnp.int32, (SW_BLOCK, LANES), 1) < HEAD_DIM

    def block_body(n, carry):
        variant = jnp.where(n == 0, 0, jnp.where(n == nb - 1, 2, 1))
        q0 = pl.multiple_of(n * SW_BLOCK, SW_BLOCK)
        for g in range(SW_KV_HEADS):
            kv_cols = slice(g * LANES, (g + 1) * LANES)
            ks = kpad_ref[pl.ds(q0, 3 * SW_BLOCK), kv_cols]
            vs = vpad_ref[pl.ds(q0, 3 * SW_BLOCK), kv_cols]
            parts = []
            for jj in range(rep // 2):
                j = g * (rep // 2) + jj
                q2 = q_ref[pl.ds(q0, SW_BLOCK), j * LANES:(j + 1) * LANES].astype(F32)
                parts.append(jnp.where(lo_half, q2, 0.0))
                parts.append(jnp.where(lo_half, 0.0, q2))
            qst = jnp.concatenate(parts, axis=0).astype(BF16)
            s = _dot_nt(qst, ks)
            ps, ls = [], []
            for i in range(rep):
                h = g * rep + i
                si = s[i * SW_BLOCK:(i + 1) * SW_BLOCK, :] + bias_ref[variant * N_HEADS + h]
                sink = sink_ref[h]
                m = jnp.maximum(jnp.max(si, axis=-1, keepdims=True), sink)
                e = jnp.exp(si - m)
                ls.append(jnp.sum(e, axis=-1, keepdims=True) + jnp.exp(sink - m))
                ps.append(e.astype(BF16))
            res = _dot(jnp.concatenate(ps, axis=0), vs)
            for jj in range(rep // 2):
                j = g * (rep // 2) + jj
                o_even = res[(2 * jj) * SW_BLOCK:(2 * jj + 1) * SW_BLOCK, :] / ls[2 * jj]
                o_odd = res[(2 * jj + 1) * SW_BLOCK:(2 * jj + 2) * SW_BLOCK, :] / ls[2 * jj + 1]
                o_ref[pl.ds(q0, SW_BLOCK), j * LANES:(j + 1) * LANES] = (
                    jnp.where(lo_half, o_even, o_odd).astype(BF16))
        return carry

    lax.fori_loop(0, nb, block_body, 0)


def _resident(shape):
    return pl.BlockSpec(shape, lambda *_: (0,) * len(shape), pipeline_mode=pl.Buffered(1))


def _layer_resident(arr, l):
    tail = arr.shape[1:]
    return pl.BlockSpec((None,) + tail, lambda *_: (l,) + (0,) * len(tail),
                        pipeline_mode=pl.Buffered(1))


def _tok(width):
    return pl.BlockSpec((TOKEN_TILE, width), lambda i: (i, 0))


def _params():
    return pltpu.CompilerParams(dimension_semantics=("arbitrary",), vmem_limit_bytes=VMEM_LIMIT)


def _ffn_inproj(l, x, n1, wg, wu, wd, mixnorm, win, bgate, bd, gains):
    t = x.shape[0]
    out_shape = (
        jax.ShapeDtypeStruct((t, D_MODEL), F32),
        jax.ShapeDtypeStruct((t, ATT_W), BF16), jax.ShapeDtypeStruct((t, ATT_W), BF16),
        jax.ShapeDtypeStruct((t, ATT_W), BF16), jax.ShapeDtypeStruct((t, ATT_W), BF16),
        jax.ShapeDtypeStruct((t, KV_DUP_W), BF16), jax.ShapeDtypeStruct((t, KV_DUP_W), BF16),
        jax.ShapeDtypeStruct((t, 2 * D_MODEL), BF16),
    )
    layer_args = (n1, wg, wu, wd, mixnorm, win, bgate)
    return pl.pallas_call(
        _ffn_inproj_kernel,
        out_shape=out_shape,
        grid=(t // TOKEN_TILE,),
        in_specs=[_tok(D_MODEL)] + [_layer_resident(a, l) for a in layer_args]
                 + [_resident(bd.shape), _layer_resident(gains, l)],
        out_specs=(_tok(D_MODEL), _tok(ATT_W), _tok(ATT_W), _tok(ATT_W), _tok(ATT_W),
                   _tok(KV_DUP_W), _tok(KV_DUP_W), _tok(2 * D_MODEL)),
        scratch_shapes=[pltpu.VMEM((TOKEN_TILE, D_FF), BF16)],
        compiler_params=_params(),
        name="ffn_inproj",
    )(x, *layer_args, bd, gains)


def _outproj_ffn(l, x1, ona, osw, g, pa, pb, wout, n2, wg, wu, wd):
    t = x1.shape[0]
    layer_args = (pa, pb, wout, n2, wg, wu, wd)
    return pl.pallas_call(
        _outproj_ffn_kernel,
        out_shape=jax.ShapeDtypeStruct((t, D_MODEL), F32),
        grid=(t // TOKEN_TILE,),
        in_specs=[_tok(D_MODEL), _tok(ATT_W), _tok(ATT_W), _tok(2 * D_MODEL)]
                 + [_layer_resident(a, l) for a in layer_args],
        out_specs=_tok(D_MODEL),
        scratch_shapes=[pltpu.VMEM((TOKEN_TILE, D_FF), BF16)],
        compiler_params=_params(),
        name="outproj_ffn",
    )(x1, ona, osw, g, *layer_args)


def _na_attn(l, q, k, v, bias, batch, seq):
    spec = pl.BlockSpec((None, seq, ATT_W), lambda b: (b, 0, 0))
    return pl.pallas_call(
        functools.partial(_na_kernel, rows=seq // GRID_W),
        out_shape=jax.ShapeDtypeStruct((batch, seq, ATT_W), BF16),
        grid=(batch,),
        in_specs=[spec, spec, spec, _layer_resident(bias, l)],
        out_specs=spec,
        compiler_params=_params(),
        name="na_attn",
    )(q, k, v, bias)


def _sw_attn(sink, q, k, v, bias, batch, seq):
    qspec = pl.BlockSpec((None, seq, ATT_W), lambda b: (b, 0, 0))
    kvspec = pl.BlockSpec((None, seq, KV_DUP_W), lambda b: (b, 0, 0))
    return pl.pallas_call(
        functools.partial(_sw_kernel, seq=seq),
        out_shape=jax.ShapeDtypeStruct((batch, seq, ATT_W), BF16),
        grid=(batch,),
        in_specs=[pl.BlockSpec(memory_space=pltpu.SMEM), qspec, kvspec, kvspec,
                  _resident(bias.shape)],
        out_specs=qspec,
        scratch_shapes=[pltpu.VMEM((seq + 2 * SW_BLOCK, KV_DUP_W), BF16),
                        pltpu.VMEM((seq + 2 * SW_BLOCK, KV_DUP_W), BF16)],
        compiler_params=_params(),
        name="sw_attn",
    )(sink, q, k, v, bias)


def _t5_bucket(rel):
    nb = REL_BUCKETS // 2
    max_exact = nb // 2
    n = np.abs(rel)
    large = max_exact + (np.log(np.maximum(n, 1) / max_exact)
                         / np.log(REL_MAX_DIST / max_exact) * (nb - max_exact)).astype(np.int32)
    large = np.minimum(large, nb - 1)
    return ((rel > 0) * nb + np.where(n < max_exact, n, large)).astype(np.int32)


def _select_rows(table, idx, n):
    onehot = (jnp.asarray(idx.reshape(-1))[None, :] == jnp.arange(n)[:, None]).astype(F32)
    out = jnp.dot(table.astype(F32), onehot, precision=lax.Precision.HIGHEST)
    return out.reshape(table.shape[:-1] + idx.shape)


def _sw_bias_tables(t5_rel_table, seq):
    a = np.arange(SW_BLOCK)[:, None]
    j = np.arange(3 * SW_BLOCK)[None, :]
    rel = j - SW_BLOCK - a
    band = np.abs(rel) <= SW_WINDOW
    nb = seq // SW_BLOCK
    masks = []
    for n in (0, 1 if nb > 2 else 0, nb - 1):
        kpos = (n - 1) * SW_BLOCK + j
        masks.append(band & (kpos >= 0) & (kpos < seq))
    mask = np.stack(masks)[:, None]
    t5 = _select_rows(t5_rel_table.T, _t5_bucket(rel), REL_BUCKETS)
    bias = jnp.where(mask, t5[None], NEG)
    return bias.reshape(3 * N_HEADS, SW_BLOCK, 3 * SW_BLOCK)


def _na_bias_tables(rpb):
    c = np.arange(GRID_W)[:, None]
    kc = np.arange(GRID_W)[None, :]
    cidx = np.clip(kc - c + NA_WIN_COLS - 1, 0, 2 * NA_WIN_COLS - 2)
    start = np.clip(c - NA_WIN_COLS // 2, 0, GRID_W - NA_WIN_COLS)
    mask = (kc >= start) & (kc < start + NA_WIN_COLS)
    t = _select_rows(rpb, cidx, 2 * NA_WIN_COLS - 1)
    t = jnp.where(mask, t, NEG)
    depth, heads, nr = t.shape[:3]
    t = jnp.transpose(t, (0, 2, 1, 3, 4)).reshape(depth, nr, heads * GRID_W, GRID_W)
    return jnp.concatenate([t[:, :-1], t[:, 1:]], axis=-1)


def kernel(x, ffn1_norm, ffn1_w_gate, ffn1_w_up, ffn1_w_down, mix_norm, w_in, b_gate, na_q_norm, na_k_norm, na_rpb, sw_q_norm, sw_k_norm, sw_sink, t5_rel_table, w_branch_na, w_branch_sw, w_out, ffn2_norm, ffn2_w_gate, ffn2_w_up, ffn2_w_down):
    batch, seq, d_model = x.shape
    depth = w_in.shape[0]
    assert d_model == D_MODEL and seq % SW_BLOCK == 0 and seq % GRID_W == 0
    assert seq // GRID_W >= NA_WIN_ROWS and (batch * seq) % TOKEN_TILE == 0
    scale = 1.0 / math.sqrt(HEAD_DIM)
    bf = lambda w: w.astype(BF16)
    row = lambda v: v[:, None, :]

    head_of = np.arange(ATT_W) // HEAD_DIM
    bd = jnp.asarray((head_of[:, None] == head_of[None, :]) / HEAD_DIM, BF16)
    sw_bias = _sw_bias_tables(t5_rel_table, seq)
    na_bias = _na_bias_tables(na_rpb)

    a, hd = ATT_W, HEAD_DIM
    kv = w_in[:, :, 4 * a:4 * a + 4 * hd]
    dup = lambda w: jnp.concatenate([w[..., :hd], w[..., :hd], w[..., hd:], w[..., hd:]], axis=-1)
    win = bf(jnp.concatenate([w_in[:, :, :4 * a], dup(kv[..., :2 * hd]), dup(kv[..., 2 * hd:]),
                              w_in[:, :, 4 * a + 4 * hd:]], axis=-1))
    tile = lambda g: jnp.tile(g, (1, N_HEADS))
    gains = jnp.stack([tile(na_q_norm) * scale, tile(na_k_norm),
                       tile(sw_q_norm) * scale, tile(sw_k_norm)], axis=1).astype(F32)
    wg1, wu1, wd1 = bf(ffn1_w_gate), bf(ffn1_w_up), bf(ffn1_w_down)
    wg2, wu2, wd2 = bf(ffn2_w_gate), bf(ffn2_w_up), bf(ffn2_w_down)
    pa, pb, wout = bf(w_branch_na), bf(w_branch_sw), bf(w_out)
    n1, n2, nmix, bgate = row(ffn1_norm), row(ffn2_norm), row(mix_norm), row(b_gate)

    xt = x.reshape(batch * seq, D_MODEL)
    to_seq = lambda t: t.reshape(batch, seq, t.shape[-1])
    to_tok = lambda t: t.reshape(batch * seq, t.shape[-1])
    for l in range(depth):
        x1, qa, ka, va, qs, ks, vs, g = _ffn_inproj(l, xt, n1, wg1, wu1, wd1, nmix, win, bgate, bd, gains)
        ona = _na_attn(l, to_seq(qa), to_seq(ka), to_seq(va), na_bias, batch, seq)
        osw = _sw_attn(sw_sink[l], to_seq(qs), to_seq(ks), to_seq(vs), sw_bias, batch, seq)
        xt = _outproj_ffn(l, x1, to_tok(ona), to_tok(osw), g, pa, pb, wout, n2, wg2, wu2, wd2)
    return xt.reshape(batch, seq, D_MODEL)
```

```python
import functools
import math

import jax
import jax.numpy as jnp
import numpy as np
from jax import lax
from jax.experimental import pallas as pl
from jax.experimental.pallas import tpu as pltpu

F32 = jnp.float32
BF16 = jnp.bfloat16

D_MODEL = 1024
D_FF = 2816
GRID_W = 64
HEAD_DIM = 64
N_HEADS = 8
NA_WIN_ROWS = 8
NA_WIN_COLS = 16
SW_KV_HEADS = 2
SW_BLOCK = 128
SW_WINDOW = 128
REL_BUCKETS = 32
REL_MAX_DIST = 128
EPS = 1e-6
NEG = -1e30

ATT_W = N_HEADS * HEAD_DIM
KV_DUP_W = 2 * SW_KV_HEADS * HEAD_DIM
LANES = 128
FFN_CHUNK = 512
TOKEN_TILE = 512
VMEM_LIMIT = 56 * 1024 * 1024


def _dot(a, b):
    return jnp.dot(a, b, preferred_element_type=F32)


def _dot_nt(a, b):
    return lax.dot_general(a, b, (((1,), (1,)), ((), ())), preferred_element_type=F32)


def _rms(x, g):
    ms = jnp.mean(x * x, axis=-1, keepdims=True)
    return x * lax.rsqrt(ms + EPS) * g


def _ffn_update(x, norm_ref, wg_ref, wu_ref, wd_ref, act_ref):
    h = _rms(x, norm_ref[...]).astype(BF16)
    for c in range(0, D_FF, FFN_CHUNK):
        w = min(FFN_CHUNK, D_FF - c)
        g = _dot(h, wg_ref[:, c:c + w])
        u = _dot(h, wu_ref[:, c:c + w])
        act_ref[:, c:c + w] = (g * jax.nn.sigmoid(g) * u).astype(BF16)
    return x + 0.5 * _dot(act_ref[...], wd_ref[...])


def _lo_half(rows):
    return lax.broadcasted_iota(jnp.int32, (rows, LANES), 1) < HEAD_DIM


def _head_norm(z, gain):
    lo = _lo_half(z.shape[0])
    tiles = []
    for t in range(z.shape[1] // LANES):
        zt = z[:, t * LANES:(t + 1) * LANES]
        sq = zt * zt
        s_lo = jnp.sum(jnp.where(lo, sq, 0.0), axis=-1, keepdims=True)
        s_hi = jnp.sum(jnp.where(lo, 0.0, sq), axis=-1, keepdims=True)
        ms = jnp.where(lo, s_lo, s_hi) * (1.0 / HEAD_DIM)
        tiles.append(zt * lax.rsqrt(ms + EPS))
    return jnp.concatenate(tiles, axis=1) * gain


def _dup_heads(t):
    lo = _lo_half(t.shape[0])
    swapped = pltpu.roll(t, HEAD_DIM, axis=1)
    return jnp.concatenate([jnp.where(lo, t, swapped), jnp.where(lo, swapped, t)], axis=1)


def _inproj(x1, mixnorm_ref, win_ref, bgate_ref, gains_ref,
            qa_ref, ka_ref, va_ref, qs_ref, ks_ref, vs_ref, g_ref):
    h = _rms(x1, mixnorm_ref[...]).astype(BF16)
    w = ATT_W
    qa_ref[...] = _head_norm(_dot(h, win_ref[:, 0:w]), gains_ref[0:1, :]).astype(BF16)
    ka_ref[...] = _head_norm(_dot(h, win_ref[:, w:2 * w]), gains_ref[1:2, :]).astype(BF16)
    va_ref[...] = _dot(h, win_ref[:, 2 * w:3 * w]).astype(BF16)
    qs_ref[...] = _head_norm(_dot(h, win_ref[:, 3 * w:4 * w]), gains_ref[2:3, :]).astype(BF16)
    c = 4 * w
    kv = _dot(h, win_ref[:, c:c + 2 * LANES])
    ks_ref[...] = _dup_heads(_head_norm(kv[:, :LANES], gains_ref[3:4, 0:LANES])).astype(BF16)
    vs_ref[...] = _dup_heads(kv[:, LANES:]).astype(BF16)
    c += 2 * LANES
    for j in range(0, 2 * D_MODEL, FFN_CHUNK):
        zg = _dot(h, win_ref[:, c + j:c + j + FFN_CHUNK]) + bgate_ref[:, j:j + FFN_CHUNK]
        g_ref[:, j:j + FFN_CHUNK] = jax.nn.sigmoid(zg).astype(BF16)


def _ffn_inproj_kernel(x_ref, n1_ref, wg_ref, wu_ref, wd_ref, mixnorm_ref, win_ref, bgate_ref,
                       gains_ref,
                       x1_ref, qa_ref, ka_ref, va_ref, qs_ref, ks_ref, vs_ref, g_ref, act_ref):
    x1 = _ffn_update(x_ref[...], n1_ref, wg_ref, wu_ref, wd_ref, act_ref)
    x1_ref[...] = x1
    _inproj(x1, mixnorm_ref, win_ref, bgate_ref, gains_ref,
            qa_ref, ka_ref, va_ref, qs_ref, ks_ref, vs_ref, g_ref)


def _outproj_ffn_kernel(x_ref, ona_ref, osw_ref, g_ref, pa_ref, pb_ref, wout_ref,
                        n2_ref, wg_ref, wu_ref, wd_ref, o_ref, act_ref):
    ya = _dot(ona_ref[...], pa_ref[...])
    yb = _dot(osw_ref[...], pb_ref[...])
    merged = g_ref[:, 0:D_MODEL].astype(F32) * ya + g_ref[:, D_MODEL:2 * D_MODEL].astype(F32) * yb
    x2 = x_ref[...] + _dot(merged.astype(BF16), wout_ref[...])
    o_ref[...] = _ffn_update(x2, n2_ref, wg_ref, wu_ref, wd_ref, act_ref)


def _na_kernel(q_ref, k_ref, v_ref, bias_ref, o_ref, *, rows):
    kr = NA_WIN_ROWS
    nkeys = kr * GRID_W
    lo_half = lax.broadcasted_iota(jnp.int32, (GRID_W, LANES), 1) < HEAD_DIM

    def row_body(r, carry):
        row_start = jnp.clip(r - kr // 2, 0, rows - kr)
        d = r - row_start
        q0 = pl.multiple_of(r * GRID_W, GRID_W)
        k0 = pl.multiple_of(row_start * GRID_W, GRID_W)
        s_parts = []
        for hp in range(N_HEADS // 2):
            cols = slice(hp * LANES, (hp + 1) * LANES)
            q2 = q_ref[pl.ds(q0, GRID_W), cols].astype(F32)
            qst = jnp.concatenate([jnp.where(lo_half, q2, 0.0), jnp.where(lo_half, 0.0, q2)], axis=0)
            s_parts.append(_dot_nt(qst.astype(BF16), k_ref[pl.ds(k0, nkeys), cols]))
        bias = jnp.concatenate([bias_ref[NA_WIN_ROWS - 1 - d + 2 * t] for t in range(kr // 2)], axis=1)
        s = jnp.concatenate(s_parts, axis=0) + bias
        m = jnp.max(s, axis=-1, keepdims=True)
        e = jnp.exp(s - m)
        l = jnp.sum(e, axis=-1, keepdims=True)
        p = e.astype(BF16)
        for hp in range(N_HEADS // 2):
            cols = slice(hp * LANES, (hp + 1) * LANES)
            res = _dot(p[cols, :], v_ref[pl.ds(k0, nkeys), cols]) / l[cols, :]
            o_ref[pl.ds(q0, GRID_W), cols] = jnp.where(lo_half, res[:GRID_W], res[GRID_W:]).astype(BF16)
        return carry

    lax.fori_loop(0, rows, row_body, 0, unroll=2)


def _sw_kernel(sink_ref, q_ref, k_ref, v_ref, bias_ref, o_ref, kpad_ref, vpad_ref, *, seq):
    nb = seq // SW_BLOCK
    rep = N_HEADS // SW_KV_HEADS
    zeros = jnp.zeros((SW_BLOCK, KV_DUP_W), BF16)
    for pad_ref, src_ref in ((kpad_ref, k_ref), (vpad_ref, v_ref)):
        pad_ref[0:SW_BLOCK, :] = zeros
        pad_ref[SW_BLOCK + seq:2 * SW_BLOCK + seq, :] = zeros
        pad_ref[SW_BLOCK:SW_BLOCK + seq, :] = src_ref[...]
    lo_half = lax.broadcasted_iota(jnp.int32, (SW_BLOCK, LANES), 1) < HEAD_DIM

    def block_body(n, carry):
        variant = jnp.where(n == 0, 0, jnp.where(n == nb - 1, 2, 1))
        q0 = pl.multiple_of(n * SW_BLOCK, SW_BLOCK)
        for g in range(SW_KV_HEADS):
            kv_cols = slice(g * LANES, (g + 1) * LANES)
            ks = kpad_ref[pl.ds(q0, 3 * SW_BLOCK), kv_cols]
            vs = vpad_ref[pl.ds(q0, 3 * SW_BLOCK), kv_cols]
            parts = []
            for jj in range(rep // 2):
                j = g * (rep // 2) + jj
                q2 = q_ref[pl.ds(q0, SW_BLOCK), j * LANES:(j + 1) * LANES].astype(F32)
                parts.append(jnp.where(lo_half, q2, 0.0))
                parts.append(jnp.where(lo_half, 0.0, q2))
            qst = jnp.concatenate(parts, axis=0).astype(BF16)
            s = _dot_nt(qst, ks)
            ps, ls = [], []
            for i in range(rep):
                h = g * rep + i
                si = s[i * SW_BLOCK:(i + 1) * SW_BLOCK, :] + bias_ref[variant * N_HEADS + h]
                sink = sink_ref[h]
                m = jnp.maximum(jnp.max(si, axis=-1, keepdims=True), sink)
                e = jnp.exp(si - m)
                ls.append(jnp.sum(e, axis=-1, keepdims=True) + jnp.exp(sink - m))
                ps.append(e.astype(BF16))
            res = _dot(jnp.concatenate(ps, axis=0), vs)
            for jj in range(rep // 2):
                j = g * (rep // 2) + jj
                o_even = res[(2 * jj) * SW_BLOCK:(2 * jj + 1) * SW_BLOCK, :] / ls[2 * jj]
                o_odd = res[(2 * jj + 1) * SW_BLOCK:(2 * jj + 2) * SW_BLOCK, :] / ls[2 * jj + 1]
                o_ref[pl.ds(q0, SW_BLOCK), j * LANES:(j + 1) * LANES] = (
                    jnp.where(lo_half, o_even, o_odd).astype(BF16))
        return carry

    lax.fori_loop(0, nb, block_body, 0, unroll=2)


def _resident(shape):
    return pl.BlockSpec(shape, lambda *_: (0,) * len(shape), pipeline_mode=pl.Buffered(1))


def _layer_resident(arr, l):
    tail = arr.shape[1:]
    return pl.BlockSpec((None,) + tail, lambda *_: (l,) + (0,) * len(tail),
                        pipeline_mode=pl.Buffered(1))


def _tok(width):
    return pl.BlockSpec((TOKEN_TILE, width), lambda i: (i, 0))


def _params():
    return pltpu.CompilerParams(dimension_semantics=("arbitrary",), vmem_limit_bytes=VMEM_LIMIT)


def _ffn_inproj(l, x, n1, wg, wu, wd, mixnorm, win, bgate, gains):
    t = x.shape[0]
    out_shape = (
        jax.ShapeDtypeStruct((t, D_MODEL), F32),
        jax.ShapeDtypeStruct((t, ATT_W), BF16), jax.ShapeDtypeStruct((t, ATT_W), BF16),
        jax.ShapeDtypeStruct((t, ATT_W), BF16), jax.ShapeDtypeStruct((t, ATT_W), BF16),
        jax.ShapeDtypeStruct((t, KV_DUP_W), BF16), jax.ShapeDtypeStruct((t, KV_DUP_W), BF16),
        jax.ShapeDtypeStruct((t, 2 * D_MODEL), BF16),
    )
    layer_args = (n1, wg, wu, wd, mixnorm, win, bgate, gains)
    return pl.pallas_call(
        _ffn_inproj_kernel,
        out_shape=out_shape,
        grid=(t // TOKEN_TILE,),
        in_specs=[_tok(D_MODEL)] + [_layer_resident(a, l) for a in layer_args],
        out_specs=(_tok(D_MODEL), _tok(ATT_W), _tok(ATT_W), _tok(ATT_W), _tok(ATT_W),
                   _tok(KV_DUP_W), _tok(KV_DUP_W), _tok(2 * D_MODEL)),
        scratch_shapes=[pltpu.VMEM((TOKEN_TILE, D_FF), BF16)],
        compiler_params=_params(),
        name="ffn_inproj",
    )(x, *layer_args)


def _outproj_ffn(l, x1, ona, osw, g, pa, pb, wout, n2, wg, wu, wd):
    t = x1.shape[0]
    layer_args = (pa, pb, wout, n2, wg, wu, wd)
    return pl.pallas_call(
        _outproj_ffn_kernel,
        out_shape=jax.ShapeDtypeStruct((t, D_MODEL), F32),
        grid=(t // TOKEN_TILE,),
        in_specs=[_tok(D_MODEL), _tok(ATT_W), _tok(ATT_W), _tok(2 * D_MODEL)]
                 + [_layer_resident(a, l) for a in layer_args],
        out_specs=_tok(D_MODEL),
        scratch_shapes=[pltpu.VMEM((TOKEN_TILE, D_FF), BF16)],
        compiler_params=_params(),
        name="outproj_ffn",
    )(x1, ona, osw, g, *layer_args)


def _na_attn(l, q, k, v, bias, batch, seq):
    spec = pl.BlockSpec((None, seq, ATT_W), lambda b: (b, 0, 0))
    return pl.pallas_call(
        functools.partial(_na_kernel, rows=seq // GRID_W),
        out_shape=jax.ShapeDtypeStruct((batch, seq, ATT_W), BF16),
        grid=(batch,),
        in_specs=[spec, spec, spec, _layer_resident(bias, l)],
        out_specs=spec,
        compiler_params=_params(),
        name="na_attn",
    )(q, k, v, bias)


def _sw_attn(sink, q, k, v, bias, batch, seq):
    qspec = pl.BlockSpec((None, seq, ATT_W), lambda b: (b, 0, 0))
    kvspec = pl.BlockSpec((None, seq, KV_DUP_W), lambda b: (b, 0, 0))
    return pl.pallas_call(
        functools.partial(_sw_kernel, seq=seq),
        out_shape=jax.ShapeDtypeStruct((batch, seq, ATT_W), BF16),
        grid=(batch,),
        in_specs=[pl.BlockSpec(memory_space=pltpu.SMEM), qspec, kvspec, kvspec,
                  _resident(bias.shape)],
        out_specs=qspec,
        scratch_shapes=[pltpu.VMEM((seq + 2 * SW_BLOCK, KV_DUP_W), BF16),
                        pltpu.VMEM((seq + 2 * SW_BLOCK, KV_DUP_W), BF16)],
        compiler_params=_params(),
        name="sw_attn",
    )(sink, q, k, v, bias)


def _t5_bucket(rel):
    nb = REL_BUCKETS // 2
    max_exact = nb // 2
    n = np.abs(rel)
    large = max_exact + (np.log(np.maximum(n, 1) / max_exact)
                         / np.log(REL_MAX_DIST / max_exact) * (nb - max_exact)).astype(np.int32)
    large = np.minimum(large, nb - 1)
    return ((rel > 0) * nb + np.where(n < max_exact, n, large)).astype(np.int32)


def _select_rows(table, idx, n):
    onehot = (jnp.asarray(idx.reshape(-1))[None, :] == jnp.arange(n)[:, None]).astype(F32)
    out = jnp.dot(table.astype(F32), onehot, precision=lax.Precision.HIGHEST)
    return out.reshape(table.shape[:-1] + idx.shape)


def _sw_bias_tables(t5_rel_table, seq):
    a = np.arange(SW_BLOCK)[:, None]
    j = np.arange(3 * SW_BLOCK)[None, :]
    rel = j - SW_BLOCK - a
    band = np.abs(rel) <= SW_WINDOW
    nb = seq // SW_BLOCK
    masks = []
    for n in (0, 1 if nb > 2 else 0, nb - 1):
        kpos = (n - 1) * SW_BLOCK + j
        masks.append(band & (kpos >= 0) & (kpos < seq))
    mask = np.stack(masks)[:, None]
    t5 = _select_rows(t5_rel_table.T, _t5_bucket(rel), REL_BUCKETS)
    bias = jnp.where(mask, t5[None], NEG)
    return bias.reshape(3 * N_HEADS, SW_BLOCK, 3 * SW_BLOCK)


def _na_bias_tables(rpb):
    c = np.arange(GRID_W)[:, None]
    kc = np.arange(GRID_W)[None, :]
    cidx = np.clip(kc - c + NA_WIN_COLS - 1, 0, 2 * NA_WIN_COLS - 2)
    start = np.clip(c - NA_WIN_COLS // 2, 0, GRID_W - NA_WIN_COLS)
    mask = (kc >= start) & (kc < start + NA_WIN_COLS)
    t = _select_rows(rpb, cidx, 2 * NA_WIN_COLS - 1)
    t = jnp.where(mask, t, NEG)
    depth, heads, nr = t.shape[:3]
    t = jnp.transpose(t, (0, 2, 1, 3, 4)).reshape(depth, nr, heads * GRID_W, GRID_W)
    return jnp.concatenate([t[:, :-1], t[:, 1:]], axis=-1)


def kernel(x, ffn1_norm, ffn1_w_gate, ffn1_w_up, ffn1_w_down, mix_norm, w_in, b_gate, na_q_norm, na_k_norm, na_rpb, sw_q_norm, sw_k_norm, sw_sink, t5_rel_table, w_branch_na, w_branch_sw, w_out, ffn2_norm, ffn2_w_gate, ffn2_w_up, ffn2_w_down):
    batch, seq, d_model = x.shape
    depth = w_in.shape[0]
    assert d_model == D_MODEL and seq % SW_BLOCK == 0 and seq % GRID_W == 0
    assert seq // GRID_W >= NA_WIN_ROWS and (batch * seq) % TOKEN_TILE == 0
    scale = 1.0 / math.sqrt(HEAD_DIM)
    bf = lambda w: w.astype(BF16)
    row = lambda v: v[:, None, :]

    sw_bias = _sw_bias_tables(t5_rel_table, seq)
    na_bias = _na_bias_tables(na_rpb)

    win = bf(w_in)
    tile = lambda g: jnp.tile(g, (1, N_HEADS))
    gains = jnp.stack([tile(na_q_norm) * scale, tile(na_k_norm),
                       tile(sw_q_norm) * scale, tile(sw_k_norm)], axis=1).astype(F32)
    wg1, wu1, wd1 = bf(ffn1_w_gate), bf(ffn1_w_up), bf(ffn1_w_down)
    wg2, wu2, wd2 = bf(ffn2_w_gate), bf(ffn2_w_up), bf(ffn2_w_down)
    pa, pb, wout = bf(w_branch_na), bf(w_branch_sw), bf(w_out)
    n1, n2, nmix, bgate = row(ffn1_norm), row(ffn2_norm), row(mix_norm), row(b_gate)

    xt = x.reshape(batch * seq, D_MODEL)
    to_seq = lambda t: t.reshape(batch, seq, t.shape[-1])
    to_tok = lambda t: t.reshape(batch * seq, t.shape[-1])
    for l in range(depth):
        x1, qa, ka, va, qs, ks, vs, g = _ffn_inproj(l, xt, n1, wg1, wu1, wd1, nmix, win, bgate, gains)
        ona = _na_attn(l, to_seq(qa), to_seq(ka), to_seq(va), na_bias, batch, seq)
        osw = _sw_attn(sw_sink[l], to_seq(qs), to_seq(ks), to_seq(vs), sw_bias, batch, seq)
        xt = _outproj_ffn(l, x1, to_tok(ona), to_tok(osw), g, pa, pb, wout, n2, wg2, wu2, wd2)
    return xt.reshape(batch, seq, D_MODEL)
```

```python
import functools
import math

import jax
import jax.numpy as jnp
import numpy as np
from jax import lax
from jax.experimental import pallas as pl
from jax.experimental.pallas import tpu as pltpu

F32 = jnp.float32
BF16 = jnp.bfloat16

D_MODEL = 1024
D_FF = 2816
GRID_W = 64
HEAD_DIM = 64
N_HEADS = 8
NA_WIN_ROWS = 8
NA_WIN_COLS = 16
SW_KV_HEADS = 2
SW_BLOCK = 128
SW_WINDOW = 128
REL_BUCKETS = 32
REL_MAX_DIST = 128
EPS = 1e-6
NEG = -1e30

ATT_W = N_HEADS * HEAD_DIM
KV_DUP_W = 2 * SW_KV_HEADS * HEAD_DIM
LANES = 128
FFN_CHUNK = 512
TOKEN_TILE = 512
VMEM_LIMIT = 56 * 1024 * 1024


def _dot(a, b):
    return jnp.dot(a, b, preferred_element_type=F32)


def _dot_nt(a, b):
    return lax.dot_general(a, b, (((1,), (1,)), ((), ())), preferred_element_type=F32)


def _rms(x, g):
    ms = jnp.mean(x * x, axis=-1, keepdims=True)
    return x * lax.rsqrt(ms + EPS) * g


def _ffn_update(x, norm_ref, wg_ref, wu_ref, wd_ref, act_ref):
    h = _rms(x, norm_ref[...]).astype(BF16)
    for c in range(0, D_FF, FFN_CHUNK):
        w = min(FFN_CHUNK, D_FF - c)
        g = _dot(h, wg_ref[:, c:c + w])
        u = _dot(h, wu_ref[:, c:c + w])
        act_ref[:, c:c + w] = (g * jax.nn.sigmoid(g) * u).astype(BF16)
    return x + 0.5 * _dot(act_ref[...], wd_ref[...])


def _lo_half(rows):
    return lax.broadcasted_iota(jnp.int32, (rows, LANES), 1) < HEAD_DIM


def _head_norm(z, gain):
    lo = _lo_half(z.shape[0])
    tiles = []
    for t in range(z.shape[1] // LANES):
        zt = z[:, t * LANES:(t + 1) * LANES]
        sq = zt * zt
        s_lo = jnp.sum(jnp.where(lo, sq, 0.0), axis=-1, keepdims=True)
        s_hi = jnp.sum(jnp.where(lo, 0.0, sq), axis=-1, keepdims=True)
        ms = jnp.where(lo, s_lo, s_hi) * (1.0 / HEAD_DIM)
        tiles.append(zt * lax.rsqrt(ms + EPS))
    return jnp.concatenate(tiles, axis=1) * gain


def _dup_heads(t):
    lo = _lo_half(t.shape[0])
    swapped = pltpu.roll(t, HEAD_DIM, axis=1)
    return jnp.concatenate([jnp.where(lo, t, swapped), jnp.where(lo, swapped, t)], axis=1)


def _inproj(x1, mixnorm_ref, win_ref, bgate_ref, gains_ref,
            qa_ref, ka_ref, va_ref, qs_ref, ks_ref, vs_ref, g_ref):
    h = _rms(x1, mixnorm_ref[...]).astype(BF16)
    w = ATT_W
    qa_ref[...] = _head_norm(_dot(h, win_ref[:, 0:w]), gains_ref[0:1, :]).astype(BF16)
    ka_ref[...] = _head_norm(_dot(h, win_ref[:, w:2 * w]), gains_ref[1:2, :]).astype(BF16)
    va_ref[...] = _dot(h, win_ref[:, 2 * w:3 * w]).astype(BF16)
    qs_ref[...] = _head_norm(_dot(h, win_ref[:, 3 * w:4 * w]), gains_ref[2:3, :]).astype(BF16)
    c = 4 * w
    kv = _dot(h, win_ref[:, c:c + 2 * LANES])
    ks_ref[...] = _dup_heads(_head_norm(kv[:, :LANES], gains_ref[3:4, 0:LANES])).astype(BF16)
    vs_ref[...] = _dup_heads(kv[:, LANES:]).astype(BF16)
    c += 2 * LANES
    for j in range(0, 2 * D_MODEL, FFN_CHUNK):
        zg = _dot(h, win_ref[:, c + j:c + j + FFN_CHUNK]) + bgate_ref[:, j:j + FFN_CHUNK]
        g_ref[:, j:j + FFN_CHUNK] = jax.nn.sigmoid(zg).astype(BF16)


def _ffn_inproj_kernel(x_ref, n1_ref, wg_ref, wu_ref, wd_ref, mixnorm_ref, win_ref, bgate_ref,
                       gains_ref,
                       x1_ref, qa_ref, ka_ref, va_ref, qs_ref, ks_ref, vs_ref, g_ref, act_ref):
    x1 = _ffn_update(x_ref[...], n1_ref, wg_ref, wu_ref, wd_ref, act_ref)
    x1_ref[...] = x1
    _inproj(x1, mixnorm_ref, win_ref, bgate_ref, gains_ref,
            qa_ref, ka_ref, va_ref, qs_ref, ks_ref, vs_ref, g_ref)


def _outproj_ffn_kernel(x_ref, ona_ref, osw_ref, g_ref, pa_ref, pb_ref, wout_ref,
                        n2_ref, wg_ref, wu_ref, wd_ref, o_ref, act_ref):
    ya = _dot(ona_ref[...], pa_ref[...])
    yb = _dot(osw_ref[...], pb_ref[...])
    merged = g_ref[:, 0:D_MODEL].astype(F32) * ya + g_ref[:, D_MODEL:2 * D_MODEL].astype(F32) * yb
    x2 = x_ref[...] + _dot(merged.astype(BF16), wout_ref[...])
    o_ref[...] = _ffn_update(x2, n2_ref, wg_ref, wu_ref, wd_ref, act_ref)


def _na_row(r, q_ref, k_ref, v_ref, bias_ref, o_ref, rows):
    kr = NA_WIN_ROWS
    nkeys = kr * GRID_W
    lo_half = _lo_half(GRID_W)
    row_start = jnp.clip(r - kr // 2, 0, rows - kr)
    d = r - row_start
    q0 = pl.multiple_of(r * GRID_W, GRID_W)
    k0 = pl.multiple_of(row_start * GRID_W, GRID_W)
    s_parts = []
    for hp in range(N_HEADS // 2):
        cols = slice(hp * LANES, (hp + 1) * LANES)
        q2 = q_ref[pl.ds(q0, GRID_W), cols].astype(F32)
        qst = jnp.concatenate([jnp.where(lo_half, q2, 0.0), jnp.where(lo_half, 0.0, q2)], axis=0)
        s_parts.append(_dot_nt(qst.astype(BF16), k_ref[pl.ds(k0, nkeys), cols]))
    bias = jnp.concatenate([bias_ref[NA_WIN_ROWS - 1 - d + 2 * t] for t in range(kr // 2)], axis=1)
    s = jnp.concatenate(s_parts, axis=0) + bias
    m = jnp.max(s, axis=-1, keepdims=True)
    e = jnp.exp(s - m)
    l = jnp.sum(e, axis=-1, keepdims=True)
    p = e.astype(BF16)
    for hp in range(N_HEADS // 2):
        cols = slice(hp * LANES, (hp + 1) * LANES)
        res = _dot(p[cols, :], v_ref[pl.ds(k0, nkeys), cols]) / l[cols, :]
        o_ref[pl.ds(q0, GRID_W), cols] = jnp.where(lo_half, res[:GRID_W], res[GRID_W:]).astype(BF16)


def _sw_block(n, sink_ref, q_ref, kpad_ref, vpad_ref, bias_ref, o_ref, nb):
    rep = N_HEADS // SW_KV_HEADS
    lo_half = _lo_half(SW_BLOCK)
    variant = jnp.where(n == 0, 0, jnp.where(n == nb - 1, 2, 1))
    q0 = pl.multiple_of(n * SW_BLOCK, SW_BLOCK)
    for g in range(SW_KV_HEADS):
        kv_cols = slice(g * LANES, (g + 1) * LANES)
        ks = kpad_ref[pl.ds(q0, 3 * SW_BLOCK), kv_cols]
        vs = vpad_ref[pl.ds(q0, 3 * SW_BLOCK), kv_cols]
        parts = []
        for jj in range(rep // 2):
            j = g * (rep // 2) + jj
            q2 = q_ref[pl.ds(q0, SW_BLOCK), j * LANES:(j + 1) * LANES].astype(F32)
            parts.append(jnp.where(lo_half, q2, 0.0))
            parts.append(jnp.where(lo_half, 0.0, q2))
        qst = jnp.concatenate(parts, axis=0).astype(BF16)
        s = _dot_nt(qst, ks)
        ps, ls = [], []
        for i in range(rep):
            h = g * rep + i
            si = s[i * SW_BLOCK:(i + 1) * SW_BLOCK, :] + bias_ref[variant * N_HEADS + h]
            sink = sink_ref[h]
            m = jnp.maximum(jnp.max(si, axis=-1, keepdims=True), sink)
            e = jnp.exp(si - m)
            ls.append(jnp.sum(e, axis=-1, keepdims=True) + jnp.exp(sink - m))
            ps.append(e.astype(BF16))
        res = _dot(jnp.concatenate(ps, axis=0), vs)
        for jj in range(rep // 2):
            j = g * (rep // 2) + jj
            o_even = res[(2 * jj) * SW_BLOCK:(2 * jj + 1) * SW_BLOCK, :] / ls[2 * jj]
            o_odd = res[(2 * jj + 1) * SW_BLOCK:(2 * jj + 2) * SW_BLOCK, :] / ls[2 * jj + 1]
            o_ref[pl.ds(q0, SW_BLOCK), j * LANES:(j + 1) * LANES] = (
                jnp.where(lo_half, o_even, o_odd).astype(BF16))


def _attn_kernel(sink_ref, qa_ref, ka_ref, va_ref, nabias_ref, qs_ref, ks_ref, vs_ref, swbias_ref,
                 ona_ref, osw_ref, kpad_ref, vpad_ref, *, seq):
    rows = seq // GRID_W
    nb = seq // SW_BLOCK
    rows_per_block = SW_BLOCK // GRID_W
    zeros = jnp.zeros((SW_BLOCK, KV_DUP_W), BF16)
    for pad_ref, src_ref in ((kpad_ref, ks_ref), (vpad_ref, vs_ref)):
        pad_ref[0:SW_BLOCK, :] = zeros
        pad_ref[SW_BLOCK + seq:2 * SW_BLOCK + seq, :] = zeros
        pad_ref[SW_BLOCK:SW_BLOCK + seq, :] = src_ref[...]

    def body(n, carry):
        for i in range(rows_per_block):
            _na_row(n * rows_per_block + i, qa_ref, ka_ref, va_ref, nabias_ref, ona_ref, rows)
        _sw_block(n, sink_ref, qs_ref, kpad_ref, vpad_ref, swbias_ref, osw_ref, nb)
        return carry

    lax.fori_loop(0, nb, body, 0)


def _resident(shape):
    return pl.BlockSpec(shape, lambda *_: (0,) * len(shape), pipeline_mode=pl.Buffered(1))


def _layer_resident(arr, l):
    tail = arr.shape[1:]
    return pl.BlockSpec((None,) + tail, lambda *_: (l,) + (0,) * len(tail),
                        pipeline_mode=pl.Buffered(1))


def _tok(width):
    return pl.BlockSpec((TOKEN_TILE, width), lambda i: (i, 0))


def _params():
    return pltpu.CompilerParams(dimension_semantics=("arbitrary",), vmem_limit_bytes=VMEM_LIMIT)


def _ffn_inproj(l, x, n1, wg, wu, wd, mixnorm, win, bgate, gains):
    t = x.shape[0]
    out_shape = (
        jax.ShapeDtypeStruct((t, D_MODEL), F32),
        jax.ShapeDtypeStruct((t, ATT_W), BF16), jax.ShapeDtypeStruct((t, ATT_W), BF16),
        jax.ShapeDtypeStruct((t, ATT_W), BF16), jax.ShapeDtypeStruct((t, ATT_W), BF16),
        jax.ShapeDtypeStruct((t, KV_DUP_W), BF16), jax.ShapeDtypeStruct((t, KV_DUP_W), BF16),
        jax.ShapeDtypeStruct((t, 2 * D_MODEL), BF16),
    )
    layer_args = (n1, wg, wu, wd, mixnorm, win, bgate, gains)
    return pl.pallas_call(
        _ffn_inproj_kernel,
        out_shape=out_shape,
        grid=(t // TOKEN_TILE,),
        in_specs=[_tok(D_MODEL)] + [_layer_resident(a, l) for a in layer_args],
        out_specs=(_tok(D_MODEL), _tok(ATT_W), _tok(ATT_W), _tok(ATT_W), _tok(ATT_W),
                   _tok(KV_DUP_W), _tok(KV_DUP_W), _tok(2 * D_MODEL)),
        scratch_shapes=[pltpu.VMEM((TOKEN_TILE, D_FF), BF16)],
        compiler_params=_params(),
        name="ffn_inproj",
    )(x, *layer_args)


def _outproj_ffn(l, x1, ona, osw, g, pa, pb, wout, n2, wg, wu, wd):
    t = x1.shape[0]
    layer_args = (pa, pb, wout, n2, wg, wu, wd)
    return pl.pallas_call(
        _outproj_ffn_kernel,
        out_shape=jax.ShapeDtypeStruct((t, D_MODEL), F32),
        grid=(t // TOKEN_TILE,),
        in_specs=[_tok(D_MODEL), _tok(ATT_W), _tok(ATT_W), _tok(2 * D_MODEL)]
                 + [_layer_resident(a, l) for a in layer_args],
        out_specs=_tok(D_MODEL),
        scratch_shapes=[pltpu.VMEM((TOKEN_TILE, D_FF), BF16)],
        compiler_params=_params(),
        name="outproj_ffn",
    )(x1, ona, osw, g, *layer_args)


def _attn(l, sink, qa, ka, va, na_bias, qs, ks, vs, sw_bias, batch, seq):
    wide = pl.BlockSpec((None, seq, ATT_W), lambda b: (b, 0, 0))
    kvspec = pl.BlockSpec((None, seq, KV_DUP_W), lambda b: (b, 0, 0))
    out = jax.ShapeDtypeStruct((batch, seq, ATT_W), BF16)
    pad = pltpu.VMEM((seq + 2 * SW_BLOCK, KV_DUP_W), BF16)
    return pl.pallas_call(
        functools.partial(_attn_kernel, seq=seq),
        out_shape=(out, out),
        grid=(batch,),
        in_specs=[pl.BlockSpec(memory_space=pltpu.SMEM), wide, wide, wide,
                  _layer_resident(na_bias, l), wide, kvspec, kvspec, _resident(sw_bias.shape)],
        out_specs=(wide, wide),
        scratch_shapes=[pad, pad],
        compiler_params=_params(),
        name="attn",
    )(sink, qa, ka, va, na_bias, qs, ks, vs, sw_bias)


def _t5_bucket(rel):
    nb = REL_BUCKETS // 2
    max_exact = nb // 2
    n = np.abs(rel)
    large = max_exact + (np.log(np.maximum(n, 1) / max_exact)
                         / np.log(REL_MAX_DIST / max_exact) * (nb - max_exact)).astype(np.int32)
    large = np.minimum(large, nb - 1)
    return ((rel > 0) * nb + np.where(n < max_exact, n, large)).astype(np.int32)


def _select_rows(table, idx, n):
    onehot = (jnp.asarray(idx.reshape(-1))[None, :] == jnp.arange(n)[:, None]).astype(F32)
    out = jnp.dot(table.astype(F32), onehot, precision=lax.Precision.HIGHEST)
    return out.reshape(table.shape[:-1] + idx.shape)


def _sw_bias_tables(t5_rel_table, seq):
    a = np.arange(SW_BLOCK)[:, None]
    j = np.arange(3 * SW_BLOCK)[None, :]
    rel = j - SW_BLOCK - a
    band = np.abs(rel) <= SW_WINDOW
    nb = seq // SW_BLOCK
    masks = []
    for n in (0, 1 if nb > 2 else 0, nb - 1):
        kpos = (n - 1) * SW_BLOCK + j
        masks.append(band & (kpos >= 0) & (kpos < seq))
    mask = np.stack(masks)[:, None]
    t5 = _select_rows(t5_rel_table.T, _t5_bucket(rel), REL_BUCKETS)
    bias = jnp.where(mask, t5[None], NEG)
    return bias.reshape(3 * N_HEADS, SW_BLOCK, 3 * SW_BLOCK)


def _na_bias_tables(rpb):
    c = np.arange(GRID_W)[:, None]
    kc = np.arange(GRID_W)[None, :]
    cidx = np.clip(kc - c + NA_WIN_COLS - 1, 0, 2 * NA_WIN_COLS - 2)
    start = np.clip(c - NA_WIN_COLS // 2, 0, GRID_W - NA_WIN_COLS)
    mask = (kc >= start) & (kc < start + NA_WIN_COLS)
    t = _select_rows(rpb, cidx, 2 * NA_WIN_COLS - 1)
    t = jnp.where(mask, t, NEG)
    depth, heads, nr = t.shape[:3]
    t = jnp.transpose(t, (0, 2, 1, 3, 4)).reshape(depth, nr, heads * GRID_W, GRID_W)
    return jnp.concatenate([t[:, :-1], t[:, 1:]], axis=-1)


def kernel(x, ffn1_norm, ffn1_w_gate, ffn1_w_up, ffn1_w_down, mix_norm, w_in, b_gate, na_q_norm, na_k_norm, na_rpb, sw_q_norm, sw_k_norm, sw_sink, t5_rel_table, w_branch_na, w_branch_sw, w_out, ffn2_norm, ffn2_w_gate, ffn2_w_up, ffn2_w_down):
    batch, seq, d_model = x.shape
    depth = w_in.shape[0]
    assert d_model == D_MODEL and seq % SW_BLOCK == 0 and seq % GRID_W == 0
    assert seq // GRID_W >= NA_WIN_ROWS and (batch * seq) % TOKEN_TILE == 0
    scale = 1.0 / math.sqrt(HEAD_DIM)
    bf = lambda w: w.astype(BF16)
    row = lambda v: v[:, None, :]

    sw_bias = _sw_bias_tables(t5_rel_table, seq)
    na_bias = _na_bias_tables(na_rpb)

    win = bf(w_in)
    tile = lambda g: jnp.tile(g, (1, N_HEADS))
    gains = jnp.stack([tile(na_q_norm) * scale, tile(na_k_norm),
                       tile(sw_q_norm) * scale, tile(sw_k_norm)], axis=1).astype(F32)
    wg1, wu1, wd1 = bf(ffn1_w_gate), bf(ffn1_w_up), bf(ffn1_w_down)
    wg2, wu2, wd2 = bf(ffn2_w_gate), bf(ffn2_w_up), bf(ffn2_w_down)
    pa, pb, wout = bf(w_branch_na), bf(w_branch_sw), bf(w_out)
    n1, n2, nmix, bgate = row(ffn1_norm), row(ffn2_norm), row(mix_norm), row(b_gate)

    xt = x.reshape(batch * seq, D_MODEL)
    to_seq = lambda t: t.reshape(batch, seq, t.shape[-1])
    to_tok = lambda t: t.reshape(batch * seq, t.shape[-1])
    for l in range(depth):
        x1, qa, ka, va, qs, ks, vs, g = _ffn_inproj(l, xt, n1, wg1, wu1, wd1, nmix, win, bgate, gains)
        ona, osw = _attn(l, sw_sink[l], to_seq(qa), to_seq(ka), to_seq(va), na_bias,
                         to_seq(qs), to_seq(ks), to_seq(vs), sw_bias, batch, seq)
        xt = _outproj_ffn(l, x1, to_tok(ona), to_tok(osw), g, pa, pb, wout, n2, wg2, wu2, wd2)
    return xt.reshape(batch, seq, D_MODEL)
```

```python
import functools
import math

import jax
import jax.numpy as jnp
import numpy as np
from jax import lax
from jax.experimental import pallas as pl
from jax.experimental.pallas import tpu as pltpu

F32 = jnp.float32
BF16 = jnp.bfloat16

D_MODEL = 1024
D_FF = 2816
GRID_W = 64
HEAD_DIM = 64
N_HEADS = 8
NA_WIN_ROWS = 8
NA_WIN_COLS = 16
SW_KV_HEADS = 2
SW_BLOCK = 128
SW_WINDOW = 128
REL_BUCKETS = 32
REL_MAX_DIST = 128
EPS = 1e-6
NEG = -1e30
LOG2E = math.log2(math.e)

ATT_W = N_HEADS * HEAD_DIM
KV_DUP_W = 2 * SW_KV_HEADS * HEAD_DIM
LANES = 128
FFN_CHUNK = 512
TOKEN_TILE = 512
VMEM_LIMIT = 56 * 1024 * 1024


def _dot(a, b):
    return jnp.dot(a, b, preferred_element_type=F32)


def _dot_nt(a, b):
    return lax.dot_general(a, b, (((1,), (1,)), ((), ())), preferred_element_type=F32)


def _rms(x, g):
    ms = jnp.mean(x * x, axis=-1, keepdims=True)
    return x * lax.rsqrt(ms + EPS) * g


def _ffn_update(x, norm_ref, wg_ref, wu_ref, wd_ref, act_ref):
    h = _rms(x, norm_ref[...]).astype(BF16)
    for c in range(0, D_FF, FFN_CHUNK):
        w = min(FFN_CHUNK, D_FF - c)
        g = _dot(h, wg_ref[:, c:c + w])
        u = _dot(h, wu_ref[:, c:c + w])
        act_ref[:, c:c + w] = (g * jax.nn.sigmoid(g) * u).astype(BF16)
    return x + 0.5 * _dot(act_ref[...], wd_ref[...])


def _lo_half(rows):
    return lax.broadcasted_iota(jnp.int32, (rows, LANES), 1) < HEAD_DIM


def _head_norm(z, gain):
    lo = _lo_half(z.shape[0])
    tiles = []
    for t in range(z.shape[1] // LANES):
        zt = z[:, t * LANES:(t + 1) * LANES]
        sq = zt * zt
        s_lo = jnp.sum(jnp.where(lo, sq, 0.0), axis=-1, keepdims=True)
        s_hi = jnp.sum(jnp.where(lo, 0.0, sq), axis=-1, keepdims=True)
        ms = jnp.where(lo, s_lo, s_hi) * (1.0 / HEAD_DIM)
        tiles.append(zt * lax.rsqrt(ms + EPS))
    return jnp.concatenate(tiles, axis=1) * gain


def _dup_heads(t):
    lo = _lo_half(t.shape[0])
    swapped = pltpu.roll(t, HEAD_DIM, axis=1)
    return jnp.concatenate([jnp.where(lo, t, swapped), jnp.where(lo, swapped, t)], axis=1)


def _inproj(x1, mixnorm_ref, win_ref, bgate_ref, gains_ref,
            qa_ref, ka_ref, va_ref, qs_ref, ks_ref, vs_ref, g_ref):
    h = _rms(x1, mixnorm_ref[...]).astype(BF16)
    w = ATT_W
    qa_ref[...] = _head_norm(_dot(h, win_ref[:, 0:w]), gains_ref[0:1, :]).astype(BF16)
    ka_ref[...] = _head_norm(_dot(h, win_ref[:, w:2 * w]), gains_ref[1:2, :]).astype(BF16)
    va_ref[...] = _dot(h, win_ref[:, 2 * w:3 * w]).astype(BF16)
    qs_ref[...] = _head_norm(_dot(h, win_ref[:, 3 * w:4 * w]), gains_ref[2:3, :]).astype(BF16)
    c = 4 * w
    kv = _dot(h, win_ref[:, c:c + 2 * LANES])
    ks_ref[...] = _dup_heads(_head_norm(kv[:, :LANES], gains_ref[3:4, 0:LANES])).astype(BF16)
    vs_ref[...] = _dup_heads(kv[:, LANES:]).astype(BF16)
    c += 2 * LANES
    for j in range(0, 2 * D_MODEL, FFN_CHUNK):
        zg = _dot(h, win_ref[:, c + j:c + j + FFN_CHUNK]) + bgate_ref[:, j:j + FFN_CHUNK]
        g_ref[:, j:j + FFN_CHUNK] = jax.nn.sigmoid(zg).astype(BF16)


def _ffn_inproj_kernel(x_ref, n1_ref, wg_ref, wu_ref, wd_ref, mixnorm_ref, win_ref, bgate_ref,
                       gains_ref,
                       x1_ref, qa_ref, ka_ref, va_ref, qs_ref, ks_ref, vs_ref, g_ref, act_ref):
    x1 = _ffn_update(x_ref[...], n1_ref, wg_ref, wu_ref, wd_ref, act_ref)
    x1_ref[...] = x1
    _inproj(x1, mixnorm_ref, win_ref, bgate_ref, gains_ref,
            qa_ref, ka_ref, va_ref, qs_ref, ks_ref, vs_ref, g_ref)


def _outproj_ffn_kernel(x_ref, ona_ref, osw_ref, g_ref, pa_ref, pb_ref, wout_ref,
                        n2_ref, wg_ref, wu_ref, wd_ref, o_ref, act_ref):
    ya = _dot(ona_ref[...], pa_ref[...])
    yb = _dot(osw_ref[...], pb_ref[...])
    merged = g_ref[:, 0:D_MODEL].astype(F32) * ya + g_ref[:, D_MODEL:2 * D_MODEL].astype(F32) * yb
    x2 = x_ref[...] + _dot(merged.astype(BF16), wout_ref[...])
    o_ref[...] = _ffn_update(x2, n2_ref, wg_ref, wu_ref, wd_ref, act_ref)


def _na_row(r, q_ref, k_ref, v_ref, bias_ref, o_ref, rows):
    kr = NA_WIN_ROWS
    nkeys = kr * GRID_W
    lo_half = _lo_half(GRID_W)
    row_start = jnp.clip(r - kr // 2, 0, rows - kr)
    d = r - row_start
    q0 = pl.multiple_of(r * GRID_W, GRID_W)
    k0 = pl.multiple_of(row_start * GRID_W, GRID_W)
    s_parts = []
    for hp in range(N_HEADS // 2):
        cols = slice(hp * LANES, (hp + 1) * LANES)
        q2 = q_ref[pl.ds(q0, GRID_W), cols].astype(F32)
        qst = jnp.concatenate([jnp.where(lo_half, q2, 0.0), jnp.where(lo_half, 0.0, q2)], axis=0)
        s_parts.append(_dot_nt(qst.astype(BF16), k_ref[pl.ds(k0, nkeys), cols]))

    def finish():
        bias = jnp.concatenate([bias_ref[NA_WIN_ROWS - 1 - d + 2 * t] for t in range(kr // 2)], axis=1)
        s = jnp.concatenate(s_parts, axis=0) + bias
        m = jnp.max(s, axis=-1, keepdims=True)
        e = jnp.exp2(s - m)
        l = jnp.sum(e, axis=-1, keepdims=True)
        p = e.astype(BF16)
        for hp in range(N_HEADS // 2):
            cols = slice(hp * LANES, (hp + 1) * LANES)
            res = _dot(p[cols, :], v_ref[pl.ds(k0, nkeys), cols]) / l[cols, :]
            o_ref[pl.ds(q0, GRID_W), cols] = (
                jnp.where(lo_half, res[:GRID_W], res[GRID_W:]).astype(BF16))

    return finish


def _sw_block(n, g, sink_ref, q_ref, kpad_ref, vpad_ref, bias_ref, o_ref, nb):
    rep = N_HEADS // SW_KV_HEADS
    lo_half = _lo_half(SW_BLOCK)
    variant = jnp.where(n == 0, 0, jnp.where(n == nb - 1, 2, 1))
    q0 = pl.multiple_of(n * SW_BLOCK, SW_BLOCK)
    kv_cols = slice(g * LANES, (g + 1) * LANES)
    ks = kpad_ref[pl.ds(q0, 3 * SW_BLOCK), kv_cols]
    parts = []
    for jj in range(rep // 2):
        j = g * (rep // 2) + jj
        q2 = q_ref[pl.ds(q0, SW_BLOCK), j * LANES:(j + 1) * LANES].astype(F32)
        parts.append(jnp.where(lo_half, q2, 0.0))
        parts.append(jnp.where(lo_half, 0.0, q2))
    qst = jnp.concatenate(parts, axis=0).astype(BF16)
    s = _dot_nt(qst, ks)

    def finish():
        vs = vpad_ref[pl.ds(q0, 3 * SW_BLOCK), kv_cols]
        ps, ls = [], []
        for i in range(rep):
            h = g * rep + i
            si = s[i * SW_BLOCK:(i + 1) * SW_BLOCK, :] + bias_ref[variant * N_HEADS + h]
            sink = sink_ref[h]
            m = jnp.maximum(jnp.max(si, axis=-1, keepdims=True), sink)
            e = jnp.exp2(si - m)
            ls.append(jnp.sum(e, axis=-1, keepdims=True) + jnp.exp2(sink - m))
            ps.append(e.astype(BF16))
        res = _dot(jnp.concatenate(ps, axis=0), vs)
        for jj in range(rep // 2):
            j = g * (rep // 2) + jj
            o_even = res[(2 * jj) * SW_BLOCK:(2 * jj + 1) * SW_BLOCK, :] / ls[2 * jj]
            o_odd = res[(2 * jj + 1) * SW_BLOCK:(2 * jj + 2) * SW_BLOCK, :] / ls[2 * jj + 1]
            o_ref[pl.ds(q0, SW_BLOCK), j * LANES:(j + 1) * LANES] = (
                jnp.where(lo_half, o_even, o_odd).astype(BF16))

    return finish


def _attn_kernel(sink_ref, qa_ref, ka_ref, va_ref, nabias_ref, qs_ref, ks_ref, vs_ref, swbias_ref,
                 ona_ref, osw_ref, kpad_ref, vpad_ref, *, seq):
    rows = seq // GRID_W
    nb = seq // SW_BLOCK
    rows_per_block = SW_BLOCK // GRID_W
    zeros = jnp.zeros((SW_BLOCK, KV_DUP_W), BF16)
    for pad_ref, src_ref in ((kpad_ref, ks_ref), (vpad_ref, vs_ref)):
        pad_ref[0:SW_BLOCK, :] = zeros
        pad_ref[SW_BLOCK + seq:2 * SW_BLOCK + seq, :] = zeros
        pad_ref[SW_BLOCK:SW_BLOCK + seq, :] = src_ref[...]

    def body(n, carry):
        starts = [functools.partial(_na_row, n * rows_per_block + i, qa_ref, ka_ref, va_ref,
                                    nabias_ref, ona_ref, rows) for i in range(rows_per_block)]
        starts += [functools.partial(_sw_block, n, g, sink_ref, qs_ref, kpad_ref, vpad_ref,
                                     swbias_ref, osw_ref, nb) for g in range(SW_KV_HEADS)]
        pending = starts[0]()
        for start in starts[1:]:
            nxt = start()
            pending()
            pending = nxt
        pending()
        return carry

    lax.fori_loop(0, nb, body, 0)


def _resident(shape):
    return pl.BlockSpec(shape, lambda *_: (0,) * len(shape), pipeline_mode=pl.Buffered(1))


def _layer_resident(arr, l):
    tail = arr.shape[1:]
    return pl.BlockSpec((None,) + tail, lambda *_: (l,) + (0,) * len(tail),
                        pipeline_mode=pl.Buffered(1))


def _tok(width):
    return pl.BlockSpec((TOKEN_TILE, width), lambda i: (i, 0))


def _params():
    return pltpu.CompilerParams(dimension_semantics=("arbitrary",), vmem_limit_bytes=VMEM_LIMIT)


def _ffn_inproj(l, x, n1, wg, wu, wd, mixnorm, win, bgate, gains):
    t = x.shape[0]
    out_shape = (
        jax.ShapeDtypeStruct((t, D_MODEL), F32),
        jax.ShapeDtypeStruct((t, ATT_W), BF16), jax.ShapeDtypeStruct((t, ATT_W), BF16),
        jax.ShapeDtypeStruct((t, ATT_W), BF16), jax.ShapeDtypeStruct((t, ATT_W), BF16),
        jax.ShapeDtypeStruct((t, KV_DUP_W), BF16), jax.ShapeDtypeStruct((t, KV_DUP_W), BF16),
        jax.ShapeDtypeStruct((t, 2 * D_MODEL), BF16),
    )
    layer_args = (n1, wg, wu, wd, mixnorm, win, bgate, gains)
    return pl.pallas_call(
        _ffn_inproj_kernel,
        out_shape=out_shape,
        grid=(t // TOKEN_TILE,),
        in_specs=[_tok(D_MODEL)] + [_layer_resident(a, l) for a in layer_args],
        out_specs=(_tok(D_MODEL), _tok(ATT_W), _tok(ATT_W), _tok(ATT_W), _tok(ATT_W),
                   _tok(KV_DUP_W), _tok(KV_DUP_W), _tok(2 * D_MODEL)),
        scratch_shapes=[pltpu.VMEM((TOKEN_TILE, D_FF), BF16)],
        compiler_params=_params(),
        name="ffn_inproj",
    )(x, *layer_args)


def _outproj_ffn(l, x1, ona, osw, g, pa, pb, wout, n2, wg, wu, wd):
    t = x1.shape[0]
    layer_args = (pa, pb, wout, n2, wg, wu, wd)
    return pl.pallas_call(
        _outproj_ffn_kernel,
        out_shape=jax.ShapeDtypeStruct((t, D_MODEL), F32),
        grid=(t // TOKEN_TILE,),
        in_specs=[_tok(D_MODEL), _tok(ATT_W), _tok(ATT_W), _tok(2 * D_MODEL)]
                 + [_layer_resident(a, l) for a in layer_args],
        out_specs=_tok(D_MODEL),
        scratch_shapes=[pltpu.VMEM((TOKEN_TILE, D_FF), BF16)],
        compiler_params=_params(),
        name="outproj_ffn",
    )(x1, ona, osw, g, *layer_args)


def _attn(l, sink, qa, ka, va, na_bias, qs, ks, vs, sw_bias, batch, seq):
    wide = pl.BlockSpec((None, seq, ATT_W), lambda b: (b, 0, 0))
    kvspec = pl.BlockSpec((None, seq, KV_DUP_W), lambda b: (b, 0, 0))
    out = jax.ShapeDtypeStruct((batch, seq, ATT_W), BF16)
    pad = pltpu.VMEM((seq + 2 * SW_BLOCK, KV_DUP_W), BF16)
    return pl.pallas_call(
        functools.partial(_attn_kernel, seq=seq),
        out_shape=(out, out),
        grid=(batch,),
        in_specs=[pl.BlockSpec(memory_space=pltpu.SMEM), wide, wide, wide,
                  _layer_resident(na_bias, l), wide, kvspec, kvspec, _resident(sw_bias.shape)],
        out_specs=(wide, wide),
        scratch_shapes=[pad, pad],
        compiler_params=_params(),
        name="attn",
    )(sink, qa, ka, va, na_bias, qs, ks, vs, sw_bias)


def _t5_bucket(rel):
    nb = REL_BUCKETS // 2
    max_exact = nb // 2
    n = np.abs(rel)
    large = max_exact + (np.log(np.maximum(n, 1) / max_exact)
                         / np.log(REL_MAX_DIST / max_exact) * (nb - max_exact)).astype(np.int32)
    large = np.minimum(large, nb - 1)
    return ((rel > 0) * nb + np.where(n < max_exact, n, large)).astype(np.int32)


def _select_rows(table, idx, n):
    onehot = (jnp.asarray(idx.reshape(-1))[None, :] == jnp.arange(n)[:, None]).astype(F32)
    out = jnp.dot(table.astype(F32), onehot, precision=lax.Precision.HIGHEST)
    return out.reshape(table.shape[:-1] + idx.shape)


def _sw_bias_tables(t5_rel_table, seq):
    a = np.arange(SW_BLOCK)[:, None]
    j = np.arange(3 * SW_BLOCK)[None, :]
    rel = j - SW_BLOCK - a
    band = np.abs(rel) <= SW_WINDOW
    nb = seq // SW_BLOCK
    masks = []
    for n in (0, 1 if nb > 2 else 0, nb - 1):
        kpos = (n - 1) * SW_BLOCK + j
        masks.append(band & (kpos >= 0) & (kpos < seq))
    mask = np.stack(masks)[:, None]
    t5 = _select_rows(t5_rel_table.T, _t5_bucket(rel), REL_BUCKETS)
    bias = jnp.where(mask, t5[None] * LOG2E, NEG)
    return bias.reshape(3 * N_HEADS, SW_BLOCK, 3 * SW_BLOCK)


def _na_bias_tables(rpb):
    c = np.arange(GRID_W)[:, None]
    kc = np.arange(GRID_W)[None, :]
    cidx = np.clip(kc - c + NA_WIN_COLS - 1, 0, 2 * NA_WIN_COLS - 2)
    start = np.clip(c - NA_WIN_COLS // 2, 0, GRID_W - NA_WIN_COLS)
    mask = (kc >= start) & (kc < start + NA_WIN_COLS)
    t = _select_rows(rpb, cidx, 2 * NA_WIN_COLS - 1)
    t = jnp.where(mask, t * LOG2E, NEG)
    depth, heads, nr = t.shape[:3]
    t = jnp.transpose(t, (0, 2, 1, 3, 4)).reshape(depth, nr, heads * GRID_W, GRID_W)
    return jnp.concatenate([t[:, :-1], t[:, 1:]], axis=-1)


def kernel(x, ffn1_norm, ffn1_w_gate, ffn1_w_up, ffn1_w_down, mix_norm, w_in, b_gate, na_q_norm, na_k_norm, na_rpb, sw_q_norm, sw_k_norm, sw_sink, t5_rel_table, w_branch_na, w_branch_sw, w_out, ffn2_norm, ffn2_w_gate, ffn2_w_up, ffn2_w_down):
    batch, seq, d_model = x.shape
    depth = w_in.shape[0]
    assert d_model == D_MODEL and seq % SW_BLOCK == 0 and seq % GRID_W == 0
    assert seq // GRID_W >= NA_WIN_ROWS and (batch * seq) % TOKEN_TILE == 0
    scale = LOG2E / math.sqrt(HEAD_DIM)
    bf = lambda w: w.astype(BF16)
    row = lambda v: v[:, None, :]

    sw_bias = _sw_bias_tables(t5_rel_table, seq)
    na_bias = _na_bias_tables(na_rpb)

    win = bf(w_in)
    tile = lambda g: jnp.tile(g, (1, N_HEADS))
    gains = jnp.stack([tile(na_q_norm) * scale, tile(na_k_norm),
                       tile(sw_q_norm) * scale, tile(sw_k_norm)], axis=1).astype(F32)
    wg1, wu1, wd1 = bf(ffn1_w_gate), bf(ffn1_w_up), bf(ffn1_w_down)
    wg2, wu2, wd2 = bf(ffn2_w_gate), bf(ffn2_w_up), bf(ffn2_w_down)
    pa, pb, wout = bf(w_branch_na), bf(w_branch_sw), bf(w_out)
    n1, n2, nmix, bgate = row(ffn1_norm), row(ffn2_norm), row(mix_norm), row(b_gate)

    xt = x.reshape(batch * seq, D_MODEL)
    to_seq = lambda t: t.reshape(batch, seq, t.shape[-1])
    to_tok = lambda t: t.reshape(batch * seq, t.shape[-1])
    for l in range(depth):
        x1, qa, ka, va, qs, ks, vs, g = _ffn_inproj(l, xt, n1, wg1, wu1, wd1, nmix, win, bgate, gains)
        ona, osw = _attn(l, sw_sink[l] * LOG2E, to_seq(qa), to_seq(ka), to_seq(va), na_bias,
                         to_seq(qs), to_seq(ks), to_seq(vs), sw_bias, batch, seq)
        xt = _outproj_ffn(l, x1, to_tok(ona), to_tok(osw), g, pa, pb, wout, n2, wg2, wu2, wd2)
    return xt.reshape(batch, seq, D_MODEL)
```

```python
import functools
import math

import jax
import jax.numpy as jnp
import numpy as np
from jax import lax
from jax.experimental import pallas as pl
from jax.experimental.pallas import tpu as pltpu

F32 = jnp.float32
BF16 = jnp.bfloat16

D_MODEL = 1024
D_FF = 2816
GRID_W = 64
HEAD_DIM = 64
N_HEADS = 8
NA_WIN_ROWS = 8
NA_WIN_COLS = 16
SW_KV_HEADS = 2
SW_BLOCK = 128
SW_WINDOW = 128
REL_BUCKETS = 32
REL_MAX_DIST = 128
EPS = 1e-6
NEG = -1e30
LOG2E = math.log2(math.e)

ATT_W = N_HEADS * HEAD_DIM
KV_DUP_W = 2 * SW_KV_HEADS * HEAD_DIM
LANES = 128
FFN_CHUNK = 512
TOKEN_TILE = 512
VMEM_LIMIT = 56 * 1024 * 1024


def _dot(a, b):
    return jnp.dot(a, b, preferred_element_type=F32)


def _dot_nt(a, b):
    return lax.dot_general(a, b, (((1,), (1,)), ((), ())), preferred_element_type=F32)


def _rms(x, g):
    ms = jnp.mean(x * x, axis=-1, keepdims=True)
    return x * lax.rsqrt(ms + EPS) * g


def _row_halves(n):
    return [slice(0, n // 2), slice(n // 2, n)]


def _dot_by_half(halves, w):
    return jnp.concatenate([_dot(h, w) for h in halves], axis=0)


def _ffn_update(x_halves, norm_ref, wg_ref, wu_ref, wd_ref, act_ref, split_first):
    h_halves = [_rms(x, norm_ref[...]).astype(BF16) for x in x_halves]
    h = jnp.concatenate(h_halves, axis=0)
    for c in range(0, D_FF, FFN_CHUNK):
        w = min(FFN_CHUNK, D_FF - c)
        if c == 0 and split_first:
            g = _dot_by_half(h_halves, wg_ref[:, c:c + w])
            u = _dot_by_half(h_halves, wu_ref[:, c:c + w])
        else:
            g = _dot(h, wg_ref[:, c:c + w])
            u = _dot(h, wu_ref[:, c:c + w])
        act_ref[:, c:c + w] = (g * jax.nn.sigmoid(g) * u).astype(BF16)
    return [x + 0.5 * _dot(act_ref[r, :], wd_ref[...])
            for x, r in zip(x_halves, _row_halves(h.shape[0]))]


def _lo_half(rows):
    return lax.broadcasted_iota(jnp.int32, (rows, LANES), 1) < HEAD_DIM


def _head_norm(z, gain):
    lo = _lo_half(z.shape[0])
    tiles = []
    for t in range(z.shape[1] // LANES):
        zt = z[:, t * LANES:(t + 1) * LANES]
        sq = zt * zt
        s_lo = jnp.sum(jnp.where(lo, sq, 0.0), axis=-1, keepdims=True)
        s_hi = jnp.sum(jnp.where(lo, 0.0, sq), axis=-1, keepdims=True)
        ms = jnp.where(lo, s_lo, s_hi) * (1.0 / HEAD_DIM)
        tiles.append(zt * lax.rsqrt(ms + EPS))
    return jnp.concatenate(tiles, axis=1) * gain


def _dup_heads(t):
    lo = _lo_half(t.shape[0])
    swapped = pltpu.roll(t, HEAD_DIM, axis=1)
    return jnp.concatenate([jnp.where(lo, t, swapped), jnp.where(lo, swapped, t)], axis=1)


def _inproj(x1_halves, mixnorm_ref, win_ref, bgate_ref, gains_ref,
            qa_ref, ka_ref, va_ref, qs_ref, ks_ref, vs_ref, g_ref):
    h_halves = [_rms(x, mixnorm_ref[...]).astype(BF16) for x in x1_halves]
    h = jnp.concatenate(h_halves, axis=0)
    w = ATT_W
    c = 4 * w + 2 * LANES
    for j in range(0, 2 * D_MODEL, FFN_CHUNK):
        wj = win_ref[:, c + j:c + j + FFN_CHUNK]
        zg = (_dot_by_half(h_halves, wj) if j == 0 else _dot(h, wj)) + bgate_ref[:, j:j + FFN_CHUNK]
        g_ref[:, j:j + FFN_CHUNK] = jax.nn.sigmoid(zg).astype(BF16)
    qa_ref[...] = _head_norm(_dot(h, win_ref[:, 0:w]), gains_ref[0:1, :]).astype(BF16)
    ka_ref[...] = _head_norm(_dot(h, win_ref[:, w:2 * w]), gains_ref[1:2, :]).astype(BF16)
    qs_ref[...] = _head_norm(_dot(h, win_ref[:, 3 * w:4 * w]), gains_ref[2:3, :]).astype(BF16)
    c = 4 * w
    kv = _dot(h, win_ref[:, c:c + 2 * LANES])
    ks_ref[...] = _dup_heads(_head_norm(kv[:, :LANES], gains_ref[3:4, 0:LANES])).astype(BF16)
    vs_ref[...] = _dup_heads(kv[:, LANES:]).astype(BF16)
    va_ref[...] = _dot(h, win_ref[:, 2 * w:3 * w]).astype(BF16)


def _ffn_inproj_kernel(x_ref, n1_ref, wg_ref, wu_ref, wd_ref, mixnorm_ref, win_ref, bgate_ref,
                       gains_ref,
                       x1_ref, qa_ref, ka_ref, va_ref, qs_ref, ks_ref, vs_ref, g_ref, act_ref):
    halves = _row_halves(x_ref.shape[0])
    x1_halves = _ffn_update([x_ref[r, :] for r in halves], n1_ref, wg_ref, wu_ref, wd_ref, act_ref,
                            split_first=False)
    for r, x1 in zip(halves, x1_halves):
        x1_ref[r, :] = x1
    _inproj(x1_halves, mixnorm_ref, win_ref, bgate_ref, gains_ref,
            qa_ref, ka_ref, va_ref, qs_ref, ks_ref, vs_ref, g_ref)


def _outproj_ffn_kernel(x_ref, ona_ref, osw_ref, g_ref, pa_ref, pb_ref, wout_ref,
                        n2_ref, wg_ref, wu_ref, wd_ref, o_ref, act_ref):
    halves = _row_halves(x_ref.shape[0])
    ya = _dot(ona_ref[...], pa_ref[...])
    yb = _dot(osw_ref[...], pb_ref[...])
    merged = (g_ref[:, 0:D_MODEL].astype(F32) * ya
              + g_ref[:, D_MODEL:2 * D_MODEL].astype(F32) * yb).astype(BF16)
    x2_halves = [x_ref[r, :] + _dot(merged[r, :], wout_ref[...]) for r in halves]
    x3_halves = _ffn_update(x2_halves, n2_ref, wg_ref, wu_ref, wd_ref, act_ref, split_first=True)
    for r, x3 in zip(halves, x3_halves):
        o_ref[r, :] = x3


def _na_row(r, q_ref, k_ref, v_ref, bias_ref, o_ref, rows):
    kr = NA_WIN_ROWS
    nkeys = kr * GRID_W
    lo_half = _lo_half(GRID_W)
    row_start = jnp.clip(r - kr // 2, 0, rows - kr)
    d = r - row_start
    q0 = pl.multiple_of(r * GRID_W, GRID_W)
    k0 = pl.multiple_of(row_start * GRID_W, GRID_W)
    s_parts = []
    for hp in range(N_HEADS // 2):
        cols = slice(hp * LANES, (hp + 1) * LANES)
        q2 = q_ref[pl.ds(q0, GRID_W), cols].astype(F32)
        qst = jnp.concatenate([jnp.where(lo_half, q2, 0.0), jnp.where(lo_half, 0.0, q2)], axis=0)
        s_parts.append(_dot_nt(qst.astype(BF16), k_ref[pl.ds(k0, nkeys), cols]))

    def finish():
        bias = jnp.concatenate([bias_ref[NA_WIN_ROWS - 1 - d + 2 * t] for t in range(kr // 2)], axis=1)
        s = jnp.concatenate(s_parts, axis=0) + bias
        m = jnp.max(s, axis=-1, keepdims=True)
        e = jnp.exp2(s - m)
        l = jnp.sum(e, axis=-1, keepdims=True)
        p = e.astype(BF16)
        for hp in range(N_HEADS // 2):
            cols = slice(hp * LANES, (hp + 1) * LANES)
            res = _dot(p[cols, :], v_ref[pl.ds(k0, nkeys), cols]) / l[cols, :]
            o_ref[pl.ds(q0, GRID_W), cols] = (
                jnp.where(lo_half, res[:GRID_W], res[GRID_W:]).astype(BF16))

    return finish


def _sw_block(n, g, sink_ref, q_ref, kpad_ref, vpad_ref, bias_ref, o_ref, nb):
    rep = N_HEADS // SW_KV_HEADS
    lo_half = _lo_half(SW_BLOCK)
    variant = jnp.where(n == 0, 0, jnp.where(n == nb - 1, 2, 1))
    q0 = pl.multiple_of(n * SW_BLOCK, SW_BLOCK)
    kv_cols = slice(g * LANES, (g + 1) * LANES)
    ks = kpad_ref[pl.ds(q0, 3 * SW_BLOCK), kv_cols]
    parts = []
    for jj in range(rep // 2):
        j = g * (rep // 2) + jj
        q2 = q_ref[pl.ds(q0, SW_BLOCK), j * LANES:(j + 1) * LANES].astype(F32)
        parts.append(jnp.where(lo_half, q2, 0.0))
        parts.append(jnp.where(lo_half, 0.0, q2))
    qst = jnp.concatenate(parts, axis=0).astype(BF16)
    s = _dot_nt(qst, ks)

    def finish():
        vs = vpad_ref[pl.ds(q0, 3 * SW_BLOCK), kv_cols]
        ps, ls = [], []
        for i in range(rep):
            h = g * rep + i
            si = s[i * SW_BLOCK:(i + 1) * SW_BLOCK, :] + bias_ref[variant * N_HEADS + h]
            sink = sink_ref[h]
            m = jnp.maximum(jnp.max(si, axis=-1, keepdims=True), sink)
            e = jnp.exp2(si - m)
            ls.append(jnp.sum(e, axis=-1, keepdims=True) + jnp.exp2(sink - m))
            ps.append(e.astype(BF16))
        res = _dot(jnp.concatenate(ps, axis=0), vs)
        for jj in range(rep // 2):
            j = g * (rep // 2) + jj
            o_even = res[(2 * jj) * SW_BLOCK:(2 * jj + 1) * SW_BLOCK, :] / ls[2 * jj]
            o_odd = res[(2 * jj + 1) * SW_BLOCK:(2 * jj + 2) * SW_BLOCK, :] / ls[2 * jj + 1]
            o_ref[pl.ds(q0, SW_BLOCK), j * LANES:(j + 1) * LANES] = (
                jnp.where(lo_half, o_even, o_odd).astype(BF16))

    return finish


def _attn_kernel(sink_ref, qa_ref, ka_ref, va_ref, nabias_ref, qs_ref, ks_ref, vs_ref, swbias_ref,
                 ona_ref, osw_ref, kpad_ref, vpad_ref, *, seq):
    rows = seq // GRID_W
    nb = seq // SW_BLOCK
    rows_per_block = SW_BLOCK // GRID_W
    zeros = jnp.zeros((SW_BLOCK, KV_DUP_W), BF16)
    for pad_ref, src_ref in ((kpad_ref, ks_ref), (vpad_ref, vs_ref)):
        pad_ref[0:SW_BLOCK, :] = zeros
        pad_ref[SW_BLOCK + seq:2 * SW_BLOCK + seq, :] = zeros
        pad_ref[SW_BLOCK:SW_BLOCK + seq, :] = src_ref[...]

    def body(n, carry):
        starts = [functools.partial(_na_row, n * rows_per_block + i, qa_ref, ka_ref, va_ref,
                                    nabias_ref, ona_ref, rows) for i in range(rows_per_block)]
        starts += [functools.partial(_sw_block, n, g, sink_ref, qs_ref, kpad_ref, vpad_ref,
                                     swbias_ref, osw_ref, nb) for g in range(SW_KV_HEADS)]
        pending = starts[0]()
        for start in starts[1:]:
            nxt = start()
            pending()
            pending = nxt
        pending()
        return carry

    lax.fori_loop(0, nb, body, 0)


def _resident(shape):
    return pl.BlockSpec(shape, lambda *_: (0,) * len(shape), pipeline_mode=pl.Buffered(1))


def _layer_resident(arr, l):
    tail = arr.shape[1:]
    return pl.BlockSpec((None,) + tail, lambda *_: (l,) + (0,) * len(tail),
                        pipeline_mode=pl.Buffered(1))


def _tok(width):
    return pl.BlockSpec((TOKEN_TILE, width), lambda i: (i, 0))


def _params():
    return pltpu.CompilerParams(dimension_semantics=("arbitrary",), vmem_limit_bytes=VMEM_LIMIT)


def _ffn_inproj(l, x, n1, wg, wu, wd, mixnorm, win, bgate, gains):
    t = x.shape[0]
    out_shape = (
        jax.ShapeDtypeStruct((t, D_MODEL), F32),
        jax.ShapeDtypeStruct((t, ATT_W), BF16), jax.ShapeDtypeStruct((t, ATT_W), BF16),
        jax.ShapeDtypeStruct((t, ATT_W), BF16), jax.ShapeDtypeStruct((t, ATT_W), BF16),
        jax.ShapeDtypeStruct((t, KV_DUP_W), BF16), jax.ShapeDtypeStruct((t, KV_DUP_W), BF16),
        jax.ShapeDtypeStruct((t, 2 * D_MODEL), BF16),
    )
    layer_args = (n1, wg, wu, wd, mixnorm, win, bgate, gains)
    return pl.pallas_call(
        _ffn_inproj_kernel,
        out_shape=out_shape,
        grid=(t // TOKEN_TILE,),
        in_specs=[_tok(D_MODEL)] + [_layer_resident(a, l) for a in layer_args],
        out_specs=(_tok(D_MODEL), _tok(ATT_W), _tok(ATT_W), _tok(ATT_W), _tok(ATT_W),
                   _tok(KV_DUP_W), _tok(KV_DUP_W), _tok(2 * D_MODEL)),
        scratch_shapes=[pltpu.VMEM((TOKEN_TILE, D_FF), BF16)],
        compiler_params=_params(),
        name="ffn_inproj",
    )(x, *layer_args)


def _outproj_ffn(l, x1, ona, osw, g, pa, pb, wout, n2, wg, wu, wd):
    t = x1.shape[0]
    layer_args = (pa, pb, wout, n2, wg, wu, wd)
    return pl.pallas_call(
        _outproj_ffn_kernel,
        out_shape=jax.ShapeDtypeStruct((t, D_MODEL), F32),
        grid=(t // TOKEN_TILE,),
        in_specs=[_tok(D_MODEL), _tok(ATT_W), _tok(ATT_W), _tok(2 * D_MODEL)]
                 + [_layer_resident(a, l) for a in layer_args],
        out_specs=_tok(D_MODEL),
        scratch_shapes=[pltpu.VMEM((TOKEN_TILE, D_FF), BF16)],
        compiler_params=_params(),
        name="outproj_ffn",
    )(x1, ona, osw, g, *layer_args)


def _attn(l, sink, qa, ka, va, na_bias, qs, ks, vs, sw_bias, batch, seq):
    wide = pl.BlockSpec((None, seq, ATT_W), lambda b: (b, 0, 0))
    kvspec = pl.BlockSpec((None, seq, KV_DUP_W), lambda b: (b, 0, 0))
    out = jax.ShapeDtypeStruct((batch, seq, ATT_W), BF16)
    pad = pltpu.VMEM((seq + 2 * SW_BLOCK, KV_DUP_W), BF16)
    return pl.pallas_call(
        functools.partial(_attn_kernel, seq=seq),
        out_shape=(out, out),
        grid=(batch,),
        in_specs=[pl.BlockSpec(memory_space=pltpu.SMEM), wide, wide, wide,
                  _layer_resident(na_bias, l), wide, kvspec, kvspec, _resident(sw_bias.shape)],
        out_specs=(wide, wide),
        scratch_shapes=[pad, pad],
        compiler_params=_params(),
        name="attn",
    )(sink, qa, ka, va, na_bias, qs, ks, vs, sw_bias)


def _t5_bucket(rel):
    nb = REL_BUCKETS // 2
    max_exact = nb // 2
    n = np.abs(rel)
    large = max_exact + (np.log(np.maximum(n, 1) / max_exact)
                         / np.log(REL_MAX_DIST / max_exact) * (nb - max_exact)).astype(np.int32)
    large = np.minimum(large, nb - 1)
    return ((rel > 0) * nb + np.where(n < max_exact, n, large)).astype(np.int32)


def _select_rows(table, idx, n):
    onehot = (jnp.asarray(idx.reshape(-1))[None, :] == jnp.arange(n)[:, None]).astype(F32)
    out = jnp.dot(table.astype(F32), onehot, precision=lax.Precision.HIGHEST)
    return out.reshape(table.shape[:-1] + idx.shape)


def _sw_bias_tables(t5_rel_table, seq):
    a = np.arange(SW_BLOCK)[:, None]
    j = np.arange(3 * SW_BLOCK)[None, :]
    rel = j - SW_BLOCK - a
    band = np.abs(rel) <= SW_WINDOW
    nb = seq // SW_BLOCK
    masks = []
    for n in (0, 1 if nb > 2 else 0, nb - 1):
        kpos = (n - 1) * SW_BLOCK + j
        masks.append(band & (kpos >= 0) & (kpos < seq))
    mask = np.stack(masks)[:, None]
    t5 = _select_rows(t5_rel_table.T, _t5_bucket(rel), REL_BUCKETS)
    bias = jnp.where(mask, t5[None] * LOG2E, NEG)
    return bias.reshape(3 * N_HEADS, SW_BLOCK, 3 * SW_BLOCK)


def _na_bias_tables(rpb):
    c = np.arange(GRID_W)[:, None]
    kc = np.arange(GRID_W)[None, :]
    cidx = np.clip(kc - c + NA_WIN_COLS - 1, 0, 2 * NA_WIN_COLS - 2)
    start = np.clip(c - NA_WIN_COLS // 2, 0, GRID_W - NA_WIN_COLS)
    mask = (kc >= start) & (kc < start + NA_WIN_COLS)
    t = _select_rows(rpb, cidx, 2 * NA_WIN_COLS - 1)
    t = jnp.where(mask, t * LOG2E, NEG)
    depth, heads, nr = t.shape[:3]
    t = jnp.transpose(t, (0, 2, 1, 3, 4)).reshape(depth, nr, heads * GRID_W, GRID_W)
    return jnp.concatenate([t[:, :-1], t[:, 1:]], axis=-1)


def kernel(x, ffn1_norm, ffn1_w_gate, ffn1_w_up, ffn1_w_down, mix_norm, w_in, b_gate, na_q_norm, na_k_norm, na_rpb, sw_q_norm, sw_k_norm, sw_sink, t5_rel_table, w_branch_na, w_branch_sw, w_out, ffn2_norm, ffn2_w_gate, ffn2_w_up, ffn2_w_down):
    batch, seq, d_model = x.shape
    depth = w_in.shape[0]
    assert d_model == D_MODEL and seq % SW_BLOCK == 0 and seq % GRID_W == 0
    assert seq // GRID_W >= NA_WIN_ROWS and (batch * seq) % TOKEN_TILE == 0
    scale = LOG2E / math.sqrt(HEAD_DIM)
    bf = lambda w: w.astype(BF16)
    row = lambda v: v[:, None, :]

    sw_bias = _sw_bias_tables(t5_rel_table, seq)
    na_bias = _na_bias_tables(na_rpb)

    win = bf(w_in)
    tile = lambda g: jnp.tile(g, (1, N_HEADS))
    gains = jnp.stack([tile(na_q_norm) * scale, tile(na_k_norm),
                       tile(sw_q_norm) * scale, tile(sw_k_norm)], axis=1).astype(F32)
    wg1, wu1, wd1 = bf(ffn1_w_gate), bf(ffn1_w_up), bf(ffn1_w_down)
    wg2, wu2, wd2 = bf(ffn2_w_gate), bf(ffn2_w_up), bf(ffn2_w_down)
    pa, pb, wout = bf(w_branch_na), bf(w_branch_sw), bf(w_out)
    n1, n2, nmix, bgate = row(ffn1_norm), row(ffn2_norm), row(mix_norm), row(b_gate)

    xt = x.reshape(batch * seq, D_MODEL)
    to_seq = lambda t: t.reshape(batch, seq, t.shape[-1])
    to_tok = lambda t: t.reshape(batch * seq, t.shape[-1])
    for l in range(depth):
        x1, qa, ka, va, qs, ks, vs, g = _ffn_inproj(l, xt, n1, wg1, wu1, wd1, nmix, win, bgate, gains)
        ona, osw = _attn(l, sw_sink[l] * LOG2E, to_seq(qa), to_seq(ka), to_seq(va), na_bias,
                         to_seq(qs), to_seq(ks), to_seq(vs), sw_bias, batch, seq)
        xt = _outproj_ffn(l, x1, to_tok(ona), to_tok(osw), g, pa, pb, wout, n2, wg2, wu2, wd2)
    return xt.reshape(batch, seq, D_MODEL)
```

```python
import functools
import math

import jax
import jax.numpy as jnp
import numpy as np
from jax import lax
from jax.experimental import pallas as pl
from jax.experimental.pallas import tpu as pltpu

F32 = jnp.float32
BF16 = jnp.bfloat16

D_MODEL = 1024
D_FF = 2816
GRID_W = 64
HEAD_DIM = 64
N_HEADS = 8
NA_WIN_ROWS = 8
NA_WIN_COLS = 16
SW_KV_HEADS = 2
SW_BLOCK = 128
SW_WINDOW = 128
REL_BUCKETS = 32
REL_MAX_DIST = 128
EPS = 1e-6
NEG = -1e30
LOG2E = math.log2(math.e)

ATT_W = N_HEADS * HEAD_DIM
KV_DUP_W = 2 * SW_KV_HEADS * HEAD_DIM
LANES = 128
FFN_CHUNK = 512
TOKEN_TILE = 512
VMEM_LIMIT = 56 * 1024 * 1024


def _dot(a, b):
    return jnp.dot(a, b, preferred_element_type=F32)


def _dot_nt(a, b):
    return lax.dot_general(a, b, (((1,), (1,)), ((), ())), preferred_element_type=F32)


def _rms(x, g):
    ms = jnp.mean(x * x, axis=-1, keepdims=True)
    return x * lax.rsqrt(ms + EPS) * g


def _row_halves(n):
    return [slice(0, n // 2), slice(n // 2, n)]


def _dot_by_half(halves, w):
    return jnp.concatenate([_dot(h, w) for h in halves], axis=0)


def _ffn_update(x_halves, norm_ref, wg_ref, wu_ref, wd_ref, act_ref, split_first):
    h_halves = [_rms(x, norm_ref[...]).astype(BF16) for x in x_halves]
    h = jnp.concatenate(h_halves, axis=0)
    for c in range(0, D_FF, FFN_CHUNK):
        w = min(FFN_CHUNK, D_FF - c)
        if c == 0 and split_first:
            g = _dot_by_half(h_halves, wg_ref[:, c:c + w])
            u = _dot_by_half(h_halves, wu_ref[:, c:c + w])
        else:
            g = _dot(h, wg_ref[:, c:c + w])
            u = _dot(h, wu_ref[:, c:c + w])
        act_ref[:, c:c + w] = (g * jax.nn.sigmoid(g) * u).astype(BF16)
    return [x + 0.5 * _dot(act_ref[r, :], wd_ref[...])
            for x, r in zip(x_halves, _row_halves(h.shape[0]))]


def _lo_half(rows):
    return lax.broadcasted_iota(jnp.int32, (rows, LANES), 1) < HEAD_DIM


def _head_norm(z, gain):
    lo = _lo_half(z.shape[0])
    tiles = []
    for t in range(z.shape[1] // LANES):
        zt = z[:, t * LANES:(t + 1) * LANES]
        sq = zt * zt
        s_lo = jnp.sum(jnp.where(lo, sq, 0.0), axis=-1, keepdims=True)
        s_hi = jnp.sum(jnp.where(lo, 0.0, sq), axis=-1, keepdims=True)
        ms = jnp.where(lo, s_lo, s_hi) * (1.0 / HEAD_DIM)
        tiles.append(zt * lax.rsqrt(ms + EPS))
    return jnp.concatenate(tiles, axis=1) * gain


def _dup_heads(t):
    lo = _lo_half(t.shape[0])
    swapped = pltpu.roll(t, HEAD_DIM, axis=1)
    return jnp.concatenate([jnp.where(lo, t, swapped), jnp.where(lo, swapped, t)], axis=1)


def _inproj(x1_halves, mixnorm_ref, win_ref, bgate_ref, gains_ref,
            qa_ref, ka_ref, va_ref, qs_ref, ks_ref, vs_ref, g_ref):
    h_halves = [_rms(x, mixnorm_ref[...]).astype(BF16) for x in x1_halves]
    h = jnp.concatenate(h_halves, axis=0)
    w = ATT_W
    c = 4 * w + 2 * LANES
    for j in range(0, 2 * D_MODEL, FFN_CHUNK):
        wj = win_ref[:, c + j:c + j + FFN_CHUNK]
        zg = (_dot_by_half(h_halves, wj) if j == 0 else _dot(h, wj)) + bgate_ref[:, j:j + FFN_CHUNK]
        g_ref[:, j:j + FFN_CHUNK] = jax.nn.sigmoid(zg).astype(BF16)
    qa_ref[...] = _head_norm(_dot(h, win_ref[:, 0:w]), gains_ref[0:1, :]).astype(BF16)
    ka_ref[...] = _head_norm(_dot(h, win_ref[:, w:2 * w]), gains_ref[1:2, :]).astype(BF16)
    qs_ref[...] = _head_norm(_dot(h, win_ref[:, 3 * w:4 * w]), gains_ref[2:3, :]).astype(BF16)
    c = 4 * w
    kv = _dot(h, win_ref[:, c:c + 2 * LANES])
    ks_ref[...] = _dup_heads(_head_norm(kv[:, :LANES], gains_ref[3:4, 0:LANES])).astype(BF16)
    vs_ref[...] = _dup_heads(kv[:, LANES:]).astype(BF16)
    va_ref[...] = _dot(h, win_ref[:, 2 * w:3 * w]).astype(BF16)


def _ffn_inproj_kernel(x_ref, n1_ref, wg_ref, wu_ref, wd_ref, mixnorm_ref, win_ref, bgate_ref,
                       gains_ref,
                       x1_ref, qa_ref, ka_ref, va_ref, qs_ref, ks_ref, vs_ref, g_ref, act_ref):
    halves = _row_halves(x_ref.shape[0])
    x1_halves = _ffn_update([x_ref[r, :] for r in halves], n1_ref, wg_ref, wu_ref, wd_ref, act_ref,
                            split_first=False)
    for r, x1 in zip(halves, x1_halves):
        x1_ref[r, :] = x1
    _inproj(x1_halves, mixnorm_ref, win_ref, bgate_ref, gains_ref,
            qa_ref, ka_ref, va_ref, qs_ref, ks_ref, vs_ref, g_ref)


def _outproj_ffn_kernel(x_ref, ona_ref, osw_ref, g_ref, pa_ref, pb_ref, wout_ref,
                        n2_ref, wg_ref, wu_ref, wd_ref, o_ref, act_ref):
    halves = _row_halves(x_ref.shape[0])
    ya = _dot(ona_ref[...], pa_ref[...])
    yb = _dot(osw_ref[...], pb_ref[...])
    merged = (g_ref[:, 0:D_MODEL].astype(F32) * ya
              + g_ref[:, D_MODEL:2 * D_MODEL].astype(F32) * yb).astype(BF16)
    x2_halves = [x_ref[r, :] + _dot(merged[r, :], wout_ref[...]) for r in halves]
    x3_halves = _ffn_update(x2_halves, n2_ref, wg_ref, wu_ref, wd_ref, act_ref, split_first=True)
    for r, x3 in zip(halves, x3_halves):
        o_ref[r, :] = x3


def _na_row(r, q_ref, k_ref, v_ref, bias_ref, o_ref, rows):
    kr = NA_WIN_ROWS
    nkeys = kr * GRID_W
    lo_half = _lo_half(GRID_W)
    row_start = jnp.clip(r - kr // 2, 0, rows - kr)
    d = r - row_start
    q0 = pl.multiple_of(r * GRID_W, GRID_W)
    k0 = pl.multiple_of(row_start * GRID_W, GRID_W)
    s_parts = []
    for hp in range(N_HEADS // 2):
        cols = slice(hp * LANES, (hp + 1) * LANES)
        q2 = q_ref[pl.ds(q0, GRID_W), cols].astype(F32)
        qst = jnp.concatenate([jnp.where(lo_half, q2, 0.0), jnp.where(lo_half, 0.0, q2)], axis=0)
        s_parts.append(_dot_nt(qst.astype(BF16), k_ref[pl.ds(k0, nkeys), cols]))

    def finish():
        bias = jnp.concatenate([bias_ref[NA_WIN_ROWS - 1 - d + 2 * t] for t in range(kr // 2)], axis=1)
        s = jnp.concatenate(s_parts, axis=0) + bias
        m = jnp.max(s, axis=-1, keepdims=True)
        e = jnp.exp2(s - m)
        l = jnp.sum(e, axis=-1, keepdims=True)
        p = e.astype(BF16)
        for hp in range(N_HEADS // 2):
            cols = slice(hp * LANES, (hp + 1) * LANES)
            res = _dot(p[cols, :], v_ref[pl.ds(k0, nkeys), cols]) / l[cols, :]
            o_ref[pl.ds(q0, GRID_W), cols] = (
                jnp.where(lo_half, res[:GRID_W], res[GRID_W:]).astype(BF16))

    return finish


def _sw_block(n, g, sink_ref, q_ref, kpad_ref, vpad_ref, bias_ref, o_ref, nb):
    rep = N_HEADS // SW_KV_HEADS
    lo_half = _lo_half(SW_BLOCK)
    variant = jnp.where(n == 0, 0, jnp.where(n == nb - 1, 2, 1))
    q0 = pl.multiple_of(n * SW_BLOCK, SW_BLOCK)
    kv_cols = slice(g * LANES, (g + 1) * LANES)
    ks = kpad_ref[pl.ds(q0, 3 * SW_BLOCK), kv_cols]
    parts = []
    for jj in range(rep // 2):
        j = g * (rep // 2) + jj
        q2 = q_ref[pl.ds(q0, SW_BLOCK), j * LANES:(j + 1) * LANES].astype(F32)
        parts.append(jnp.where(lo_half, q2, 0.0))
        parts.append(jnp.where(lo_half, 0.0, q2))
    qst = jnp.concatenate(parts, axis=0).astype(BF16)
    s = _dot_nt(qst, ks)

    def finish():
        vs = vpad_ref[pl.ds(q0, 3 * SW_BLOCK), kv_cols]
        ps, ls = [], []
        for i in range(rep):
            h = g * rep + i
            si = s[i * SW_BLOCK:(i + 1) * SW_BLOCK, :] + bias_ref[variant * N_HEADS + h]
            sink = sink_ref[h]
            m = jnp.maximum(jnp.max(si, axis=-1, keepdims=True), sink)
            e = jnp.exp2(si - m)
            ls.append(jnp.sum(e, axis=-1, keepdims=True) + jnp.exp2(sink - m))
            ps.append(e.astype(BF16))
        res = _dot(jnp.concatenate(ps, axis=0), vs)
        for jj in range(rep // 2):
            j = g * (rep // 2) + jj
            o_even = res[(2 * jj) * SW_BLOCK:(2 * jj + 1) * SW_BLOCK, :] / ls[2 * jj]
            o_odd = res[(2 * jj + 1) * SW_BLOCK:(2 * jj + 2) * SW_BLOCK, :] / ls[2 * jj + 1]
            o_ref[pl.ds(q0, SW_BLOCK), j * LANES:(j + 1) * LANES] = (
                jnp.where(lo_half, o_even, o_odd).astype(BF16))

    return finish


def _attn_kernel(sink_ref, qa_ref, ka_ref, va_ref, nabias_ref, qs_ref, ks_ref, vs_ref, swbias_ref,
                 ona_ref, osw_ref, kpad_ref, vpad_ref, *, seq):
    rows = seq // GRID_W
    nb = seq // SW_BLOCK
    rows_per_block = SW_BLOCK // GRID_W
    zeros = jnp.zeros((SW_BLOCK, KV_DUP_W), BF16)
    for pad_ref, src_ref in ((kpad_ref, ks_ref), (vpad_ref, vs_ref)):
        pad_ref[0:SW_BLOCK, :] = zeros
        pad_ref[SW_BLOCK + seq:2 * SW_BLOCK + seq, :] = zeros
        pad_ref[SW_BLOCK:SW_BLOCK + seq, :] = src_ref[...]

    def body(n, carry):
        starts = [functools.partial(_na_row, n * rows_per_block + i, qa_ref, ka_ref, va_ref,
                                    nabias_ref, ona_ref, rows) for i in range(rows_per_block)]
        starts += [functools.partial(_sw_block, n, g, sink_ref, qs_ref, kpad_ref, vpad_ref,
                                     swbias_ref, osw_ref, nb) for g in range(SW_KV_HEADS)]
        pending = starts[0]()
        for start in starts[1:]:
            nxt = start()
            pending()
            pending = nxt
        pending()
        return carry

    lax.fori_loop(0, nb, body, 0)


def _resident(shape):
    return pl.BlockSpec(shape, lambda *_: (0,) * len(shape), pipeline_mode=pl.Buffered(1))


def _layer_resident(arr, l):
    tail = arr.shape[1:]
    return pl.BlockSpec((None,) + tail, lambda *_: (l,) + (0,) * len(tail),
                        pipeline_mode=pl.Buffered(1))


def _tok(width):
    return pl.BlockSpec((TOKEN_TILE, width), lambda i: (i, 0))


def _params():
    return pltpu.CompilerParams(dimension_semantics=("arbitrary",), vmem_limit_bytes=VMEM_LIMIT)


def _ffn_inproj(l, x, n1, wg, wu, wd, mixnorm, win, bgate, gains):
    t = x.shape[0]
    out_shape = (
        jax.ShapeDtypeStruct((t, D_MODEL), F32),
        jax.ShapeDtypeStruct((t, ATT_W), BF16), jax.ShapeDtypeStruct((t, ATT_W), BF16),
        jax.ShapeDtypeStruct((t, ATT_W), BF16), jax.ShapeDtypeStruct((t, ATT_W), BF16),
        jax.ShapeDtypeStruct((t, KV_DUP_W), BF16), jax.ShapeDtypeStruct((t, KV_DUP_W), BF16),
        jax.ShapeDtypeStruct((t, 2 * D_MODEL), BF16),
    )
    layer_args = (n1, wg, wu, wd, mixnorm, win, bgate, gains)
    return pl.pallas_call(
        _ffn_inproj_kernel,
        out_shape=out_shape,
        grid=(t // TOKEN_TILE,),
        in_specs=[_tok(D_MODEL)] + [_layer_resident(a, l) for a in layer_args],
        out_specs=(_tok(D_MODEL), _tok(ATT_W), _tok(ATT_W), _tok(ATT_W), _tok(ATT_W),
                   _tok(KV_DUP_W), _tok(KV_DUP_W), _tok(2 * D_MODEL)),
        scratch_shapes=[pltpu.VMEM((TOKEN_TILE, D_FF), BF16)],
        compiler_params=_params(),
        name="ffn_inproj",
    )(x, *layer_args)


def _outproj_ffn(l, x1, ona, osw, g, pa, pb, wout, n2, wg, wu, wd):
    t = x1.shape[0]
    layer_args = (pa, pb, wout, n2, wg, wu, wd)
    return pl.pallas_call(
        _outproj_ffn_kernel,
        out_shape=jax.ShapeDtypeStruct((t, D_MODEL), F32),
        grid=(t // TOKEN_TILE,),
        in_specs=[_tok(D_MODEL), _tok(ATT_W), _tok(ATT_W), _tok(2 * D_MODEL)]
                 + [_layer_resident(a, l) for a in layer_args],
        out_specs=_tok(D_MODEL),
        scratch_shapes=[pltpu.VMEM((TOKEN_TILE, D_FF), BF16)],
        compiler_params=_params(),
        name="outproj_ffn",
    )(x1, ona, osw, g, *layer_args)


def _attn(l, sink, qa, ka, va, na_bias, qs, ks, vs, sw_bias, batch, seq):
    wide = pl.BlockSpec((None, seq, ATT_W), lambda b: (b, 0, 0))
    kvspec = pl.BlockSpec((None, seq, KV_DUP_W), lambda b: (b, 0, 0))
    out = jax.ShapeDtypeStruct((batch, seq, ATT_W), BF16)
    pad = pltpu.VMEM((seq + 2 * SW_BLOCK, KV_DUP_W), BF16)
    return pl.pallas_call(
        functools.partial(_attn_kernel, seq=seq),
        out_shape=(out, out),
        grid=(batch,),
        in_specs=[pl.BlockSpec(memory_space=pltpu.SMEM), wide, wide, wide,
                  _layer_resident(na_bias, l), wide, kvspec, kvspec, _resident(sw_bias.shape)],
        out_specs=(wide, wide),
        scratch_shapes=[pad, pad],
        compiler_params=_params(),
        name="attn",
    )(sink, qa, ka, va, na_bias, qs, ks, vs, sw_bias)


def _t5_bucket(rel):
    nb = REL_BUCKETS // 2
    max_exact = nb // 2
    n = np.abs(rel)
    large = max_exact + (np.log(np.maximum(n, 1) / max_exact)
                         / np.log(REL_MAX_DIST / max_exact) * (nb - max_exact)).astype(np.int32)
    large = np.minimum(large, nb - 1)
    return ((rel > 0) * nb + np.where(n < max_exact, n, large)).astype(np.int32)


def _select_rows(table, idx, n):
    onehot = (jnp.asarray(idx.reshape(-1))[None, :] == jnp.arange(n)[:, None]).astype(F32)
    out = jnp.dot(table.astype(F32), onehot, precision=lax.Precision.HIGHEST)
    return out.reshape(table.shape[:-1] + idx.shape)


def _sw_bias_tables(t5_rel_table, seq):
    a = np.arange(SW_BLOCK)[:, None]
    j = np.arange(3 * SW_BLOCK)[None, :]
    rel = j - SW_BLOCK - a
    band = np.abs(rel) <= SW_WINDOW
    nb = seq // SW_BLOCK
    masks = []
    for n in (0, 1 if nb > 2 else 0, nb - 1):
        kpos = (n - 1) * SW_BLOCK + j
        masks.append(band & (kpos >= 0) & (kpos < seq))
    mask = np.stack(masks)[:, None]
    t5 = _select_rows(t5_rel_table.T, _t5_bucket(rel), REL_BUCKETS)
    bias = jnp.where(mask, t5[None] * LOG2E, NEG)
    return bias.reshape(3 * N_HEADS, SW_BLOCK, 3 * SW_BLOCK)


def _na_bias_tables(rpb):
    nj = 2 * NA_WIN_COLS - 1
    c = np.arange(GRID_W)[:, None, None]
    half = np.arange(2)[None, :, None]
    kc = np.arange(GRID_W)[None, None, :]
    cidx = np.clip(kc - c + NA_WIN_COLS - 1, 0, nj - 1) + nj * half
    start = np.clip(c - NA_WIN_COLS // 2, 0, GRID_W - NA_WIN_COLS)
    mask = np.broadcast_to((kc >= start) & (kc < start + NA_WIN_COLS), cidx.shape)
    pairs = jnp.concatenate([rpb[:, :, :-1, :], rpb[:, :, 1:, :]], axis=-1)
    pairs = jnp.transpose(pairs, (0, 2, 1, 3))
    t = _select_rows(pairs, cidx, 2 * nj)
    t = jnp.where(mask, t * LOG2E, NEG)
    depth, nr, heads = t.shape[:3]
    return t.reshape(depth, nr, heads * GRID_W, 2 * GRID_W)


def kernel(x, ffn1_norm, ffn1_w_gate, ffn1_w_up, ffn1_w_down, mix_norm, w_in, b_gate, na_q_norm, na_k_norm, na_rpb, sw_q_norm, sw_k_norm, sw_sink, t5_rel_table, w_branch_na, w_branch_sw, w_out, ffn2_norm, ffn2_w_gate, ffn2_w_up, ffn2_w_down):
    batch, seq, d_model = x.shape
    depth = w_in.shape[0]
    assert d_model == D_MODEL and seq % SW_BLOCK == 0 and seq % GRID_W == 0
    assert seq // GRID_W >= NA_WIN_ROWS and (batch * seq) % TOKEN_TILE == 0
    scale = LOG2E / math.sqrt(HEAD_DIM)
    bf = lambda w: w.astype(BF16)
    row = lambda v: v[:, None, :]

    sw_bias = _sw_bias_tables(t5_rel_table, seq)
    na_bias = _na_bias_tables(na_rpb)

    win = bf(w_in)
    gains = (jnp.stack([na_q_norm, na_k_norm, sw_q_norm, sw_k_norm], axis=1).astype(F32)
             * jnp.asarray([scale, 1.0, scale, 1.0], F32)[None, :, None])
    gains = jnp.broadcast_to(gains[:, :, None, :], gains.shape[:2] + (N_HEADS, HEAD_DIM))
    gains = gains.reshape(depth, 4, ATT_W)
    sinks = sw_sink * LOG2E
    wg1, wu1, wd1 = bf(ffn1_w_gate), bf(ffn1_w_up), bf(ffn1_w_down)
    wg2, wu2, wd2 = bf(ffn2_w_gate), bf(ffn2_w_up), bf(ffn2_w_down)
    pa, pb, wout = bf(w_branch_na), bf(w_branch_sw), bf(w_out)
    n1, n2, nmix, bgate = row(ffn1_norm), row(ffn2_norm), row(mix_norm), row(b_gate)

    xt = x.reshape(batch * seq, D_MODEL)
    to_seq = lambda t: t.reshape(batch, seq, t.shape[-1])
    to_tok = lambda t: t.reshape(batch * seq, t.shape[-1])
    for l in range(depth):
        x1, qa, ka, va, qs, ks, vs, g = _ffn_inproj(l, xt, n1, wg1, wu1, wd1, nmix, win, bgate, gains)
        ona, osw = _attn(l, sinks[l], to_seq(qa), to_seq(ka), to_seq(va), na_bias,
                         to_seq(qs), to_seq(ks), to_seq(vs), sw_bias, batch, seq)
        xt = _outproj_ffn(l, x1, to_tok(ona), to_tok(osw), g, pa, pb, wout, n2, wg2, wu2, wd2)
    return xt.reshape(batch, seq, D_MODEL)
```

```python
import functools
import math

import jax
import jax.numpy as jnp
import numpy as np
from jax import lax
from jax.experimental import pallas as pl
from jax.experimental.pallas import tpu as pltpu

F32 = jnp.float32
BF16 = jnp.bfloat16

D_MODEL = 1024
D_FF = 2816
GRID_W = 64
HEAD_DIM = 64
N_HEADS = 8
NA_WIN_ROWS = 8
NA_WIN_COLS = 16
SW_KV_HEADS = 2
SW_BLOCK = 128
SW_WINDOW = 128
REL_BUCKETS = 32
REL_MAX_DIST = 128
EPS = 1e-6
NEG = -1e30
LOG2E = math.log2(math.e)

ATT_W = N_HEADS * HEAD_DIM
KV_DUP_W = 2 * SW_KV_HEADS * HEAD_DIM
LANES = 128
BF16_SUBLANES = 16
CAST_STEPS = 16
FFN_CHUNK = 512
TOKEN_TILE = 512
VMEM_LIMIT = 56 * 1024 * 1024


def _dot(a, b):
    return jnp.dot(a, b, preferred_element_type=F32)


def _dot_nt(a, b):
    return lax.dot_general(a, b, (((1,), (1,)), ((), ())), preferred_element_type=F32)


def _rms(x, g):
    ms = jnp.mean(x * x, axis=-1, keepdims=True)
    return x * lax.rsqrt(ms + EPS) * g


def _row_halves(n):
    return [slice(0, n // 2), slice(n // 2, n)]


def _dot_by_half(halves, w):
    return jnp.concatenate([_dot(h, w) for h in halves], axis=0)


def _ffn_update(x_halves, norm_ref, wg_ref, wu_ref, wd_ref, act_ref, split_first):
    h_halves = [_rms(x, norm_ref[...]).astype(BF16) for x in x_halves]
    h = jnp.concatenate(h_halves, axis=0)
    for c in range(0, D_FF, FFN_CHUNK):
        w = min(FFN_CHUNK, D_FF - c)
        if c == 0 and split_first:
            g = _dot_by_half(h_halves, wg_ref[:, c:c + w])
            u = _dot_by_half(h_halves, wu_ref[:, c:c + w])
        else:
            g = _dot(h, wg_ref[:, c:c + w])
            u = _dot(h, wu_ref[:, c:c + w])
        act_ref[:, c:c + w] = (g * jax.nn.sigmoid(g) * u).astype(BF16)
    return [x + 0.5 * _dot(act_ref[r, :], wd_ref[...])
            for x, r in zip(x_halves, _row_halves(h.shape[0]))]


def _lo_half(rows):
    return lax.broadcasted_iota(jnp.int32, (rows, LANES), 1) < HEAD_DIM


def _head_norm(z, gain):
    lo = _lo_half(z.shape[0])
    tiles = []
    for t in range(z.shape[1] // LANES):
        zt = z[:, t * LANES:(t + 1) * LANES]
        sq = zt * zt
        s_lo = jnp.sum(jnp.where(lo, sq, 0.0), axis=-1, keepdims=True)
        s_hi = jnp.sum(jnp.where(lo, 0.0, sq), axis=-1, keepdims=True)
        ms = jnp.where(lo, s_lo, s_hi) * (1.0 / HEAD_DIM)
        tiles.append(zt * lax.rsqrt(ms + EPS))
    return jnp.concatenate(tiles, axis=1) * gain


def _dup_heads(t):
    lo = _lo_half(t.shape[0])
    swapped = pltpu.roll(t, HEAD_DIM, axis=1)
    return jnp.concatenate([jnp.where(lo, t, swapped), jnp.where(lo, swapped, t)], axis=1)


def _inproj(x1_halves, mixnorm_ref, win_ref, bgate_ref, gains_ref,
            qa_ref, ka_ref, va_ref, qs_ref, ks_ref, vs_ref, g_ref):
    h_halves = [_rms(x, mixnorm_ref[...]).astype(BF16) for x in x1_halves]
    h = jnp.concatenate(h_halves, axis=0)
    w = ATT_W
    c = 4 * w + 2 * LANES
    for j in range(0, 2 * D_MODEL, FFN_CHUNK):
        wj = win_ref[:, c + j:c + j + FFN_CHUNK]
        zg = (_dot_by_half(h_halves, wj) if j == 0 else _dot(h, wj)) + bgate_ref[:, j:j + FFN_CHUNK]
        g_ref[:, j:j + FFN_CHUNK] = jax.nn.sigmoid(zg).astype(BF16)
    qa_ref[...] = _head_norm(_dot(h, win_ref[:, 0:w]), gains_ref[0:1, :]).astype(BF16)
    ka_ref[...] = _head_norm(_dot(h, win_ref[:, w:2 * w]), gains_ref[1:2, :]).astype(BF16)
    qs_ref[...] = _head_norm(_dot(h, win_ref[:, 3 * w:4 * w]), gains_ref[2:3, :]).astype(BF16)
    c = 4 * w
    kv = _dot(h, win_ref[:, c:c + 2 * LANES])
    ks_ref[...] = _dup_heads(_head_norm(kv[:, :LANES], gains_ref[3:4, 0:LANES])).astype(BF16)
    vs_ref[...] = _dup_heads(kv[:, LANES:]).astype(BF16)
    va_ref[...] = _dot(h, win_ref[:, 2 * w:3 * w]).astype(BF16)


def _cast_chunks(src_refs, dst_refs):
    for src, dst in zip(src_refs, dst_refs):
        dst[...] = src[...].astype(BF16)


def _cast_kernel(*refs):
    n = len(refs) // 2
    _cast_chunks(refs[:n], refs[n:])


def _ffn_inproj_kernel(*refs, n_cast):
    (x_ref, n1_ref, wg_ref, wu_ref, wd_ref, mixnorm_ref, win_ref, bgate_ref, gains_ref) = refs[:9]
    cast_src = refs[9:9 + n_cast]
    (x1_ref, qa_ref, ka_ref, va_ref, qs_ref, ks_ref, vs_ref, g_ref) = refs[9 + n_cast:17 + n_cast]
    cast_dst = refs[17 + n_cast:17 + 2 * n_cast]
    act_ref = refs[17 + 2 * n_cast]
    _cast_chunks(cast_src, cast_dst)
    halves = _row_halves(x_ref.shape[0])
    x1_halves = _ffn_update([x_ref[r, :] for r in halves], n1_ref, wg_ref, wu_ref, wd_ref, act_ref,
                            split_first=False)
    for r, x1 in zip(halves, x1_halves):
        x1_ref[r, :] = x1
    _inproj(x1_halves, mixnorm_ref, win_ref, bgate_ref, gains_ref,
            qa_ref, ka_ref, va_ref, qs_ref, ks_ref, vs_ref, g_ref)


def _outproj_ffn_kernel(*refs, n_cast):
    (x_ref, ona_ref, osw_ref, g_ref, pa_ref, pb_ref, wout_ref,
     n2_ref, wg_ref, wu_ref, wd_ref) = refs[:11]
    cast_src = refs[11:11 + n_cast]
    o_ref = refs[11 + n_cast]
    cast_dst = refs[12 + n_cast:12 + 2 * n_cast]
    act_ref = refs[12 + 2 * n_cast]
    _cast_chunks(cast_src, cast_dst)
    halves = _row_halves(x_ref.shape[0])
    ya = _dot(ona_ref[...], pa_ref[...])
    yb = _dot(osw_ref[...], pb_ref[...])
    merged = (g_ref[:, 0:D_MODEL].astype(F32) * ya
              + g_ref[:, D_MODEL:2 * D_MODEL].astype(F32) * yb).astype(BF16)
    x2_halves = [x_ref[r, :] + _dot(merged[r, :], wout_ref[...]) for r in halves]
    x3_halves = _ffn_update(x2_halves, n2_ref, wg_ref, wu_ref, wd_ref, act_ref, split_first=True)
    for r, x3 in zip(halves, x3_halves):
        o_ref[r, :] = x3


def _na_row(r, q_ref, k_ref, v_ref, bias_ref, o_ref, rows):
    kr = NA_WIN_ROWS
    nkeys = kr * GRID_W
    lo_half = _lo_half(GRID_W)
    row_start = jnp.clip(r - kr // 2, 0, rows - kr)
    d = r - row_start
    q0 = pl.multiple_of(r * GRID_W, GRID_W)
    k0 = pl.multiple_of(row_start * GRID_W, GRID_W)
    s_parts = []
    for hp in range(N_HEADS // 2):
        cols = slice(hp * LANES, (hp + 1) * LANES)
        q2 = q_ref[pl.ds(q0, GRID_W), cols].astype(F32)
        qst = jnp.concatenate([jnp.where(lo_half, q2, 0.0), jnp.where(lo_half, 0.0, q2)], axis=0)
        s_parts.append(_dot_nt(qst.astype(BF16), k_ref[pl.ds(k0, nkeys), cols]))

    def finish():
        bias = jnp.concatenate([bias_ref[NA_WIN_ROWS - 1 - d + 2 * t] for t in range(kr // 2)], axis=1)
        s = jnp.concatenate(s_parts, axis=0) + bias
        m = jnp.max(s, axis=-1, keepdims=True)
        e = jnp.exp2(s - m)
        l = jnp.sum(e, axis=-1, keepdims=True)
        p = e.astype(BF16)
        for hp in range(N_HEADS // 2):
            cols = slice(hp * LANES, (hp + 1) * LANES)
            res = _dot(p[cols, :], v_ref[pl.ds(k0, nkeys), cols]) / l[cols, :]
            o_ref[pl.ds(q0, GRID_W), cols] = (
                jnp.where(lo_half, res[:GRID_W], res[GRID_W:]).astype(BF16))

    return finish


def _sw_block(n, g, sink_ref, q_ref, kpad_ref, vpad_ref, bias_ref, o_ref, nb):
    rep = N_HEADS // SW_KV_HEADS
    lo_half = _lo_half(SW_BLOCK)
    variant = jnp.where(n == 0, 0, jnp.where(n == nb - 1, 2, 1))
    q0 = pl.multiple_of(n * SW_BLOCK, SW_BLOCK)
    kv_cols = slice(g * LANES, (g + 1) * LANES)
    ks = kpad_ref[pl.ds(q0, 3 * SW_BLOCK), kv_cols]
    parts = []
    for jj in range(rep // 2):
        j = g * (rep // 2) + jj
        q2 = q_ref[pl.ds(q0, SW_BLOCK), j * LANES:(j + 1) * LANES].astype(F32)
        parts.append(jnp.where(lo_half, q2, 0.0))
        parts.append(jnp.where(lo_half, 0.0, q2))
    qst = jnp.concatenate(parts, axis=0).astype(BF16)
    s = _dot_nt(qst, ks)

    def finish():
        vs = vpad_ref[pl.ds(q0, 3 * SW_BLOCK), kv_cols]
        ps, ls = [], []
        for i in range(rep):
            h = g * rep + i
            si = s[i * SW_BLOCK:(i + 1) * SW_BLOCK, :] + bias_ref[variant * N_HEADS + h]
            sink = sink_ref[h]
            m = jnp.maximum(jnp.max(si, axis=-1, keepdims=True), sink)
            e = jnp.exp2(si - m)
            ls.append(jnp.sum(e, axis=-1, keepdims=True) + jnp.exp2(sink - m))
            ps.append(e.astype(BF16))
        res = _dot(jnp.concatenate(ps, axis=0), vs)
        for jj in range(rep // 2):
            j = g * (rep // 2) + jj
            o_even = res[(2 * jj) * SW_BLOCK:(2 * jj + 1) * SW_BLOCK, :] / ls[2 * jj]
            o_odd = res[(2 * jj + 1) * SW_BLOCK:(2 * jj + 2) * SW_BLOCK, :] / ls[2 * jj + 1]
            o_ref[pl.ds(q0, SW_BLOCK), j * LANES:(j + 1) * LANES] = (
                jnp.where(lo_half, o_even, o_odd).astype(BF16))

    return finish


def _attn_kernel(sink_ref, qa_ref, ka_ref, va_ref, nabias_ref, qs_ref, ks_ref, vs_ref, swbias_ref,
                 ona_ref, osw_ref, kpad_ref, vpad_ref, *, seq):
    rows = seq // GRID_W
    nb = seq // SW_BLOCK
    rows_per_block = SW_BLOCK // GRID_W
    zeros = jnp.zeros((SW_BLOCK, KV_DUP_W), BF16)
    for pad_ref, src_ref in ((kpad_ref, ks_ref), (vpad_ref, vs_ref)):
        pad_ref[0:SW_BLOCK, :] = zeros
        pad_ref[SW_BLOCK + seq:2 * SW_BLOCK + seq, :] = zeros
        pad_ref[SW_BLOCK:SW_BLOCK + seq, :] = src_ref[...]

    def body(n, carry):
        starts = [functools.partial(_na_row, n * rows_per_block + i, qa_ref, ka_ref, va_ref,
                                    nabias_ref, ona_ref, rows) for i in range(rows_per_block)]
        starts += [functools.partial(_sw_block, n, g, sink_ref, qs_ref, kpad_ref, vpad_ref,
                                     swbias_ref, osw_ref, nb) for g in range(SW_KV_HEADS)]
        pending = starts[0]()
        for start in starts[1:]:
            nxt = start()
            pending()
            pending = nxt
        pending()
        return carry

    lax.fori_loop(0, nb, body, 0)


def _resident(shape):
    return pl.BlockSpec(shape, lambda *_: (0,) * len(shape), pipeline_mode=pl.Buffered(1))


def _layer_resident(arr, l):
    tail = arr.shape[1:]
    return pl.BlockSpec((None,) + tail, lambda *_: (l,) + (0,) * len(tail),
                        pipeline_mode=pl.Buffered(1))


def _tok(width):
    return pl.BlockSpec((TOKEN_TILE, width), lambda i: (i, 0))


def _params():
    return pltpu.CompilerParams(dimension_semantics=("arbitrary",), vmem_limit_bytes=VMEM_LIMIT)


def _cast_plan(weights, l, steps):
    in_specs, out_specs, out_shapes = [], [], []
    for w in weights:
        _, r, c = w.shape
        chunks = steps
        while r % (chunks * BF16_SUBLANES):
            chunks //= 2
        rows, rep = r // chunks, steps // chunks
        in_specs.append(pl.BlockSpec((None, rows, c), lambda i, rep=rep: (l, i // rep, 0)))
        out_specs.append(pl.BlockSpec((rows, c), lambda i, rep=rep: (i // rep, 0)))
        out_shapes.append(jax.ShapeDtypeStruct((r, c), BF16))
    return in_specs, out_specs, out_shapes


def _cast_weights(weights, l):
    in_specs, out_specs, out_shapes = _cast_plan(weights, l, CAST_STEPS)
    return pl.pallas_call(
        _cast_kernel,
        out_shape=tuple(out_shapes),
        grid=(CAST_STEPS,),
        in_specs=in_specs,
        out_specs=tuple(out_specs),
        compiler_params=_params(),
        name="cast_weights",
    )(*weights)


def _ffn_inproj(l, x, n1, weights, mixnorm, bgate, gains, cast, cast_layer):
    t = x.shape[0]
    steps = t // TOKEN_TILE
    wg, wu, wd, win = weights
    out_shape = (
        jax.ShapeDtypeStruct((t, D_MODEL), F32),
        jax.ShapeDtypeStruct((t, ATT_W), BF16), jax.ShapeDtypeStruct((t, ATT_W), BF16),
        jax.ShapeDtypeStruct((t, ATT_W), BF16), jax.ShapeDtypeStruct((t, ATT_W), BF16),
        jax.ShapeDtypeStruct((t, KV_DUP_W), BF16), jax.ShapeDtypeStruct((t, KV_DUP_W), BF16),
        jax.ShapeDtypeStruct((t, 2 * D_MODEL), BF16),
    )
    cast_in, cast_out, cast_shapes = _cast_plan(cast, cast_layer, steps)
    outs = pl.pallas_call(
        functools.partial(_ffn_inproj_kernel, n_cast=len(cast)),
        out_shape=out_shape + tuple(cast_shapes),
        grid=(steps,),
        in_specs=[_tok(D_MODEL), _layer_resident(n1, l), _resident(wg.shape), _resident(wu.shape),
                  _resident(wd.shape), _layer_resident(mixnorm, l), _resident(win.shape),
                  _layer_resident(bgate, l), _layer_resident(gains, l)] + cast_in,
        out_specs=(_tok(D_MODEL), _tok(ATT_W), _tok(ATT_W), _tok(ATT_W), _tok(ATT_W),
                   _tok(KV_DUP_W), _tok(KV_DUP_W), _tok(2 * D_MODEL)) + tuple(cast_out),
        scratch_shapes=[pltpu.VMEM((TOKEN_TILE, D_FF), BF16)],
        compiler_params=_params(),
        name="ffn_inproj",
    )(x, n1, wg, wu, wd, mixnorm, win, bgate, gains, *cast)
    return outs[:len(out_shape)], outs[len(out_shape):]


def _outproj_ffn(l, x1, ona, osw, g, weights, n2, cast, cast_layer):
    t = x1.shape[0]
    steps = t // TOKEN_TILE
    pa, pb, wout, wg, wu, wd = weights
    cast_in, cast_out, cast_shapes = _cast_plan(cast, cast_layer, steps)
    outs = pl.pallas_call(
        functools.partial(_outproj_ffn_kernel, n_cast=len(cast)),
        out_shape=(jax.ShapeDtypeStruct((t, D_MODEL), F32),) + tuple(cast_shapes),
        grid=(steps,),
        in_specs=[_tok(D_MODEL), _tok(ATT_W), _tok(ATT_W), _tok(2 * D_MODEL),
                  _resident(pa.shape), _resident(pb.shape), _resident(wout.shape),
                  _layer_resident(n2, l), _resident(wg.shape), _resident(wu.shape),
                  _resident(wd.shape)] + cast_in,
        out_specs=(_tok(D_MODEL),) + tuple(cast_out),
        scratch_shapes=[pltpu.VMEM((TOKEN_TILE, D_FF), BF16)],
        compiler_params=_params(),
        name="outproj_ffn",
    )(x1, ona, osw, g, pa, pb, wout, n2, wg, wu, wd, *cast)
    return outs[0], outs[1:]


def _attn(l, sink, qa, ka, va, na_bias, qs, ks, vs, sw_bias, batch, seq):
    wide = pl.BlockSpec((None, seq, ATT_W), lambda b: (b, 0, 0))
    kvspec = pl.BlockSpec((None, seq, KV_DUP_W), lambda b: (b, 0, 0))
    out = jax.ShapeDtypeStruct((batch, seq, ATT_W), BF16)
    pad = pltpu.VMEM((seq + 2 * SW_BLOCK, KV_DUP_W), BF16)
    return pl.pallas_call(
        functools.partial(_attn_kernel, seq=seq),
        out_shape=(out, out),
        grid=(batch,),
        in_specs=[pl.BlockSpec(memory_space=pltpu.SMEM), wide, wide, wide,
                  _layer_resident(na_bias, l), wide, kvspec, kvspec, _resident(sw_bias.shape)],
        out_specs=(wide, wide),
        scratch_shapes=[pad, pad],
        compiler_params=_params(),
        name="attn",
    )(sink, qa, ka, va, na_bias, qs, ks, vs, sw_bias)


def _t5_bucket(rel):
    nb = REL_BUCKETS // 2
    max_exact = nb // 2
    n = np.abs(rel)
    large = max_exact + (np.log(np.maximum(n, 1) / max_exact)
                         / np.log(REL_MAX_DIST / max_exact) * (nb - max_exact)).astype(np.int32)
    large = np.minimum(large, nb - 1)
    return ((rel > 0) * nb + np.where(n < max_exact, n, large)).astype(np.int32)


def _select_rows(table, idx, n):
    onehot = (jnp.asarray(idx.reshape(-1))[None, :] == jnp.arange(n)[:, None]).astype(F32)
    out = jnp.dot(table.astype(F32), onehot, precision=lax.Precision.HIGHEST)
    return out.reshape(table.shape[:-1] + idx.shape)


def _sw_bias_tables(t5_rel_table, seq):
    a = np.arange(SW_BLOCK)[:, None]
    j = np.arange(3 * SW_BLOCK)[None, :]
    rel = j - SW_BLOCK - a
    band = np.abs(rel) <= SW_WINDOW
    nb = seq // SW_BLOCK
    masks = []
    for n in (0, 1 if nb > 2 else 0, nb - 1):
        kpos = (n - 1) * SW_BLOCK + j
        masks.append(band & (kpos >= 0) & (kpos < seq))
    mask = np.stack(masks)[:, None]
    t5 = _select_rows(t5_rel_table.T, _t5_bucket(rel), REL_BUCKETS)
    bias = jnp.where(mask, t5[None] * LOG2E, NEG)
    return bias.reshape(3 * N_HEADS, SW_BLOCK, 3 * SW_BLOCK)


def _na_bias_tables(rpb):
    nj = 2 * NA_WIN_COLS - 1
    c = np.arange(GRID_W)[:, None, None]
    half = np.arange(2)[None, :, None]
    kc = np.arange(GRID_W)[None, None, :]
    cidx = np.clip(kc - c + NA_WIN_COLS - 1, 0, nj - 1) + nj * half
    start = np.clip(c - NA_WIN_COLS // 2, 0, GRID_W - NA_WIN_COLS)
    mask = np.broadcast_to((kc >= start) & (kc < start + NA_WIN_COLS), cidx.shape)
    cidx, mask = cidx.reshape(GRID_W, 2 * GRID_W), mask.reshape(GRID_W, 2 * GRID_W)
    pairs = jnp.concatenate([rpb[:, :, :-1, :], rpb[:, :, 1:, :]], axis=-1)
    pairs = jnp.transpose(pairs, (0, 2, 1, 3))
    t = _select_rows(pairs, cidx, 2 * nj)
    t = jnp.where(mask, t * LOG2E, NEG)
    depth, nr, heads = t.shape[:3]
    return t.reshape(depth, nr, heads * GRID_W, 2 * GRID_W)


def kernel(x, ffn1_norm, ffn1_w_gate, ffn1_w_up, ffn1_w_down, mix_norm, w_in, b_gate, na_q_norm, na_k_norm, na_rpb, sw_q_norm, sw_k_norm, sw_sink, t5_rel_table, w_branch_na, w_branch_sw, w_out, ffn2_norm, ffn2_w_gate, ffn2_w_up, ffn2_w_down):
    batch, seq, d_model = x.shape
    depth = w_in.shape[0]
    assert d_model == D_MODEL and seq % SW_BLOCK == 0 and seq % GRID_W == 0
    assert seq // GRID_W >= NA_WIN_ROWS and (batch * seq) % TOKEN_TILE == 0
    scale = LOG2E / math.sqrt(HEAD_DIM)
    row = lambda v: v[:, None, :]

    sw_bias = _sw_bias_tables(t5_rel_table, seq)
    na_bias = _na_bias_tables(na_rpb)

    gains =(jnp.stack([na_q_norm, na_k_norm, sw_q_norm, sw_k_norm], axis=1).astype(F32)
             * jnp.asarray([scale, 1.0, scale, 1.0], F32)[None, :, None])
    gains = jnp.broadcast_to(gains[:, :, None, :], gains.shape[:2] + (N_HEADS, HEAD_DIM))
    gains = gains.reshape(depth, 4, ATT_W)
    sinks = sw_sink * LOG2E
    n1, n2, nmix, bgate = row(ffn1_norm), row(ffn2_norm), row(mix_norm), row(b_gate)
    inproj_f32 = (ffn1_w_gate, ffn1_w_up, ffn1_w_down, w_in)
    outproj_f32 = (w_branch_na, w_branch_sw, w_out, ffn2_w_gate, ffn2_w_up, ffn2_w_down)
    inproj_w = _cast_weights(inproj_f32, 0)

    xt = x.reshape(batch * seq, D_MODEL)
    to_seq = lambda t: t.reshape(batch, seq, t.shape[-1])
    to_tok = lambda t: t.reshape(batch * seq, t.shape[-1])
    for l in range(depth):
        (x1, qa, ka, va, qs, ks, vs, g), outproj_w = _ffn_inproj(
            l, xt, n1, inproj_w, nmix, bgate, gains, outproj_f32, l)
        ona, osw = _attn(l, sinks[l], to_seq(qa), to_seq(ka), to_seq(va), na_bias,
                         to_seq(qs), to_seq(ks), to_seq(vs), sw_bias, batch, seq)
        ahead = inproj_f32 if l + 1 < depth else ()
        xt, inproj_w = _outproj_ffn(l, x1, to_tok(ona), to_tok(osw), g, outproj_w, n2, ahead, l + 1)
    return xt.reshape(batch, seq, D_MODEL)
```
